```python
import math
import jax, jax.numpy as jnp
from jax import lax
import numpy as np

D_MODEL = 1024
BATCH = 32
SEQ = 2048
DEPTH = 1
DEC_BATCH = 128
DEC_SEQ = 1
PAST_LEN = 8192
PAGE_SIZE = 128

ATTN_HEADS = 8
ATTN_HEAD_DIM = 64
ATTN_WIDTH = ATTN_HEADS * ATTN_HEAD_DIM
MOBA_BLOCK = 256
MOBA_TOPK = 3
QUERY_BLOCK = 128
N_BUCKETS = 32
MAX_DISTANCE = 128
HGRN_HEADS = 4
HGRN_KEY_DIM = 128
HGRN_VAL_DIM = 128
HGRN_F_WIDTH = HGRN_HEADS * HGRN_KEY_DIM
HGRN_I_WIDTH = HGRN_HEADS * HGRN_VAL_DIM
HGRN_CHUNK = 64
D_FF = 4 * D_MODEL
N_ADA = 6
EPS = 1e-6
IN_SPLITS = (ATTN_WIDTH, ATTN_WIDTH, ATTN_WIDTH, HGRN_F_WIDTH, HGRN_F_WIDTH, HGRN_I_WIDTH, HGRN_I_WIDTH, D_MODEL, D_MODEL)
IN_WIDTH = 3 * ATTN_WIDTH + 2 * HGRN_F_WIDTH + 2 * HGRN_I_WIDTH + 2 * D_MODEL

kernel_name = 'hybrid_moba_hgrn2_step'


def rmsnorm(x, w):
    xf = x.astype(jnp.float32)
    y = xf * lax.rsqrt(jnp.mean(jnp.square(xf), axis=-1, keepdims=True) + EPS)
    return (y * w.astype(jnp.float32)).astype(x.dtype)


def modulate(h, shift, scale):
    return h * (1 + scale[:, None, :]) + shift[:, None, :]


def rel_bucket(dist):
    n = jnp.maximum(dist, 0)
    max_exact = N_BUCKETS // 2
    nf = jnp.maximum(n, max_exact).astype(jnp.float32)
    large = max_exact + (jnp.log(nf / max_exact) / math.log(MAX_DISTANCE / max_exact) * (N_BUCKETS - max_exact)).astype(jnp.int32)
    large = jnp.minimum(large, N_BUCKETS - 1)
    return jnp.where(n < max_exact, n, large)


def moba_select(q, means, n_past, n_sel):
    score = jnp.einsum('nthd,nbhd->nthb', q.astype(jnp.float32), means)
    nb = means.shape[1]
    past = jnp.arange(nb)[None, None, None, :] < n_past[None, :, None, None]
    score = jnp.where(past, score, -jnp.inf)
    _, idx = lax.top_k(score, n_sel)
    valid = idx < n_past[None, :, None, None]
    return idx, valid


def moba_core(q, q_pos, k_sel, v_sel, sel_blk, sel_valid, k_own, v_own, own_pos, rel_bias):
    n, t = q.shape[:2]
    n_sel = k_sel.shape[3]
    scale = ATTN_HEAD_DIM ** -0.5
    head5 = jnp.arange(ATTN_HEADS)[None, None, :, None, None]
    sel_pos = sel_blk[..., None] * MOBA_BLOCK + jnp.arange(MOBA_BLOCK)
    b_sel = rel_bias[rel_bucket(q_pos[None, :, None, None, None] - sel_pos), head5]
    s_sel = jnp.einsum('nthd,nthkpd->nthkp', q, k_sel).astype(jnp.float32) * scale + b_sel
    s_sel = jnp.where(sel_valid[..., None], s_sel, -jnp.inf).reshape(n, t, ATTN_HEADS, n_sel * MOBA_BLOCK)
    b_own = rel_bias[rel_bucket(q_pos[:, None] - own_pos[None, :])].transpose(0, 2, 1)
    s_own = jnp.einsum('nthd,nlhd->nthl', q, k_own).astype(jnp.float32) * scale + b_own[None]
    causal = (own_pos[None, :] <= q_pos[:, None])[None, :, None, :]
    s_own = jnp.where(causal, s_own, -jnp.inf)
    p = jax.nn.softmax(jnp.concatenate([s_sel, s_own], axis=-1), axis=-1)
    p_sel = p[..., :n_sel * MOBA_BLOCK].reshape(n, t, ATTN_HEADS, n_sel, MOBA_BLOCK).astype(v_sel.dtype)
    p_own = p[..., n_sel * MOBA_BLOCK:].astype(v_own.dtype)
    return jnp.einsum('nthkp,nthkpd->nthd', p_sel, v_sel) + jnp.einsum('nthl,nlhd->nthd', p_own, v_own)


def moba_prompt(q, k, v, rel_bias):
    b, s = q.shape[:2]
    nb = -(-s // MOBA_BLOCK)
    pad = nb * MOBA_BLOCK - s
    to_blocks = lambda a: jnp.pad(a, ((0, 0), (0, pad), (0, 0), (0, 0))).reshape(b, nb, MOBA_BLOCK, ATTN_HEADS, ATTN_HEAD_DIM)
    kb, vb = to_blocks(k), to_blocks(v)
    means = jnp.mean(kb.astype(jnp.float32), axis=2)
    n_sel = min(MOBA_TOPK, nb)
    nq = s // QUERY_BLOCK
    qb = q.reshape(b, nq, QUERY_BLOCK, ATTN_HEADS, ATTN_HEAD_DIM)
    head4 = jnp.arange(ATTN_HEADS)[None, None, :, None]

    def per_seq(args):
        q_s, k_s, v_s, mu = args

        def per_block(c):
            q_c = lax.dynamic_index_in_dim(q_s, c, 0, keepdims=True)
            q_pos = c * QUERY_BLOCK + jnp.arange(QUERY_BLOCK)
            own = (c * QUERY_BLOCK) // MOBA_BLOCK
            idx, valid = moba_select(q_c, mu[None], jnp.full((QUERY_BLOCK,), own), n_sel)
            k_sel = k_s[idx, :, head4, :]
            v_sel = v_s[idx, :, head4, :]
            k_own = lax.dynamic_index_in_dim(k_s, own, 0, keepdims=True)
            v_own = lax.dynamic_index_in_dim(v_s, own, 0, keepdims=True)
            own_pos = own * MOBA_BLOCK + jnp.arange(MOBA_BLOCK)
            return moba_core(q_c, q_pos, k_sel, v_sel, idx, valid, k_own, v_own, own_pos, rel_bias)[0]

        return lax.map(per_block, jnp.arange(nq)).reshape(s, ATTN_HEADS, ATTN_HEAD_DIM)

    return lax.map(per_seq, (qb, kb, vb, means))


def moba_sample(q, k_new, v_new, cache_k, cache_v, page_table, rel_bias):
    db, t = q.shape[:2]
    ppb = MOBA_BLOCK // PAGE_SIZE
    n_full = PAST_LEN // MOBA_BLOCK
    tail = PAST_LEN - n_full * MOBA_BLOCK
    q_pos = PAST_LEN + jnp.arange(t)
    if n_full > 0:
        n_sel = min(MOBA_TOPK, n_full)
        page_sum = jnp.sum(cache_k, axis=1, dtype=jnp.float32)
        means = page_sum[page_table[:, :n_full * ppb]].reshape(db, n_full, ppb, ATTN_HEADS, ATTN_HEAD_DIM).sum(2) / MOBA_BLOCK
        idx, valid = moba_select(q, means, jnp.full((t,), n_full), n_sel)
        logical = idx[..., None] * ppb + jnp.arange(ppb)
        phys = page_table[jnp.arange(db)[:, None, None, None, None], logical]
        head5 = jnp.arange(ATTN_HEADS)[None, None, :, None, None]
        sel_shape = (db, t, ATTN_HEADS, n_sel, MOBA_BLOCK, ATTN_HEAD_DIM)
        k_sel = cache_k[phys, :, head5, :].reshape(sel_shape)
        v_sel = cache_v[phys, :, head5, :].reshape(sel_shape)
    else:
        idx = jnp.zeros((db, t, ATTN_HEADS, 0), jnp.int32)
        valid = jnp.zeros((db, t, ATTN_HEADS, 0), bool)
        k_sel = jnp.zeros((db, t, ATTN_HEADS, 0, MOBA_BLOCK, ATTN_HEAD_DIM), q.dtype)
        v_sel = k_sel
    if tail > 0:
        tp = page_table[:, n_full * ppb: n_full * ppb + tail // PAGE_SIZE]
        k_own = jnp.concatenate([cache_k[tp].reshape(db, tail, ATTN_HEADS, ATTN_HEAD_DIM), k_new], axis=1)
        v_own = jnp.concatenate([cache_v[tp].reshape(db, tail, ATTN_HEADS, ATTN_HEAD_DIM), v_new], axis=1)
    else:
        k_own, v_own = k_new, v_new
    own_pos = n_full * MOBA_BLOCK + jnp.arange(tail + t)
    return moba_core(q, q_pos, k_sel, v_sel, idx, valid, k_own, v_own, own_pos, rel_bias)


def hgrn_recurrence(q, k, v, logf, s0, chunk):
    n, t = q.shape[:2]
    nc = t // chunk
    to_chunks = lambda a: a.reshape(n, nc, chunk, *a.shape[2:]).swapaxes(0, 1)
    tri = (jnp.arange(chunk)[:, None] >= jnp.arange(chunk)[None, :])[None, :, :, None, None]

    def step(state, xs):
        qc, kc, vc, gc = xs
        b = jnp.cumsum(gc, axis=1)
        decay = jnp.exp(jnp.where(tri, b[:, :, None] - b[:, None, :], -jnp.inf))
        attn = jnp.einsum('nthk,nshk,ntshk->nhts', qc, kc, decay)
        o = jnp.einsum('nhts,nshv->nthv', attn, vc) + jnp.einsum('nthk,nhkv->nthv', qc * jnp.exp(b), state)
        b_last = b[:, -1]
        state = jnp.exp(b_last)[..., None] * state + jnp.einsum('nshk,nshv->nhkv', kc * jnp.exp(b_last[:, None] - b), vc)
        return state, o

    s_final, o = lax.scan(step, s0, (to_chunks(q), to_chunks(k), to_chunks(v), to_chunks(logf)))
    return o.swapaxes(0, 1).reshape(n, t, HGRN_HEADS, HGRN_VAL_DIM), s_final


def hgrn_branch(q_raw, f_raw, i_raw, og_raw, lb, gnorm_w, s0):
    n, t = q_raw.shape[:2]
    heads = lambda a, d: a.astype(jnp.float32).reshape(n, t, HGRN_HEADS, d)
    lbh = lb.reshape(HGRN_HEADS, HGRN_KEY_DIM)
    q = jax.nn.silu(heads(q_raw, HGRN_KEY_DIM))
    f = lbh + (1 - lbh) * jax.nn.sigmoid(heads(f_raw, HGRN_KEY_DIM))
    v = heads(i_raw, HGRN_VAL_DIM)
    chunk = HGRN_CHUNK if t % HGRN_CHUNK == 0 else t
    o, s_new = hgrn_recurrence(q, 1 - f, v, jnp.log(f), s0.astype(jnp.float32), chunk)
    o = rmsnorm(o, gnorm_w).reshape(n, t, HGRN_I_WIDTH) * jax.nn.silu(og_raw.astype(jnp.float32))
    return o.astype(q_raw.dtype), s_new.astype(s0.dtype)


def trunk_layer(x, c, attn_fn, s0, lb, w_ada, b_ada, norm_mix_w, w_in, gnorm_w, w_up_attn, w_up_hgrn, w_out, norm_ffn_w, w_ff1, w_ff2):
    n, t = x.shape[:2]
    ada = jax.nn.silu(c) @ w_ada + b_ada
    sh_m, sc_m, g_m, sh_f, sc_f, g_f = jnp.split(ada, N_ADA, axis=-1)
    h = modulate(rmsnorm(x, norm_mix_w), sh_m, sc_m)
    parts = jnp.split(h @ w_in, np.cumsum(IN_SPLITS)[:-1].tolist(), axis=-1)
    qa, ka, va, qh, fh, ih, oh, ga, gh = parts
    to_heads = lambda a: a.reshape(n, t, ATTN_HEADS, ATTN_HEAD_DIM)
    qa, ka, va = to_heads(qa), to_heads(ka), to_heads(va)
    o_attn = attn_fn(qa, ka, va).reshape(n, t, ATTN_WIDTH)
    o_hgrn, s_new = hgrn_branch(qh, fh, ih, oh, lb, gnorm_w, s0)
    merged = jax.nn.sigmoid(ga) * (o_attn @ w_up_attn) + jax.nn.sigmoid(gh) * (o_hgrn @ w_up_hgrn)
    x = x + g_m[:, None, :] * (merged @ w_out)
    h = modulate(rmsnorm(x, norm_ffn_w), sh_f, sc_f)
    x = x + g_f[:, None, :] * (jnp.square(jax.nn.relu(h @ w_ff1)) @ w_ff2)
    return x, ka, va, s_new


def setup_inputs(seed: int = 0) -> dict:
    key = jax.random.key(seed)
    ks = jax.random.split(key, 24)
    f32 = jnp.float32
    nrm = lambda k, shape, s: jax.random.normal(k, shape, f32) * s
    n_pages = PAST_LEN // PAGE_SIZE
    n_used = DEC_BATCH * n_pages
    n_phys = n_used + max(1, n_used // 4)
    pool_shape = (DEPTH, n_phys, PAGE_SIZE, ATTN_HEADS, ATTN_HEAD_DIM)
    perm = jax.random.permutation(ks[5], n_phys)
    return {
        'x_prompt': nrm(ks[0], (BATCH, SEQ, D_MODEL), 1.0),
        'x_sample': nrm(ks[1], (DEC_BATCH, DEC_SEQ, D_MODEL), 1.0),
        'cache_k': nrm(ks[2], pool_shape, 1.0),
        'cache_v': nrm(ks[3], pool_shape, 1.0),
        'state_hgrn': nrm(ks[4], (DEPTH, DEC_BATCH, HGRN_HEADS, HGRN_KEY_DIM, HGRN_VAL_DIM), 0.5),
        'page_table': perm[:n_used].reshape(DEC_BATCH, n_pages).astype(jnp.int32),
        'c_prompt': nrm(ks[6], (BATCH, D_MODEL), 1.0),
        'c_sample': nrm(ks[7], (DEC_BATCH, D_MODEL), 1.0),
        'rel_bias': nrm(ks[8], (N_BUCKETS, ATTN_HEADS), 0.5),
        'hgrn_lb_logits': nrm(ks[9], (DEPTH + 1, HGRN_F_WIDTH), 0.5),
        'w_ada': nrm(ks[10], (DEPTH, D_MODEL, N_ADA * D_MODEL), 0.5 * D_MODEL ** -0.5),
        'b_ada': nrm(ks[11], (DEPTH, N_ADA * D_MODEL), 0.02),
        'norm_mix_w': 1.0 + nrm(ks[12], (DEPTH, D_MODEL), 0.02),
        'w_in': nrm(ks[13], (DEPTH, D_MODEL, IN_WIDTH), D_MODEL ** -0.5),
        'hgrn_gnorm_w': 1.0 + nrm(ks[14], (DEPTH, HGRN_VAL_DIM), 0.02),
        'w_up_attn': nrm(ks[15], (DEPTH, ATTN_WIDTH, D_MODEL), ATTN_WIDTH ** -0.5),
        'w_up_hgrn': nrm(ks[16], (DEPTH, HGRN_I_WIDTH, D_MODEL), HGRN_I_WIDTH ** -0.5),
        'w_out': nrm(ks[17], (DEPTH, D_MODEL, D_MODEL), D_MODEL ** -0.5),
        'norm_ffn_w': 1.0 + nrm(ks[18], (DEPTH, D_MODEL), 0.02),
        'w_ff1': nrm(ks[19], (DEPTH, D_MODEL, D_FF), D_MODEL ** -0.5),
        'w_ff2': nrm(ks[20], (DEPTH, D_FF, D_MODEL), D_FF ** -0.5),
        'final_norm_w': 1.0 + nrm(ks[21], (D_MODEL,), 0.02),
    }


def reference(x_prompt, x_sample, cache_k, cache_v, state_hgrn, page_table, c_prompt, c_sample, rel_bias, hgrn_lb_logits,
              w_ada, b_ada, norm_mix_w, w_in, hgrn_gnorm_w, w_up_attn, w_up_hgrn, w_out, norm_ffn_w, w_ff1, w_ff2, final_norm_w):
    lb_all = jnp.cumsum(jax.nn.softmax(hgrn_lb_logits.astype(jnp.float32), axis=0), axis=0)
    xp, xs = x_prompt, x_sample
    kp_list, vp_list, sp_list, ks_list, vs_list, ss_list = [], [], [], [], [], []
    s0_prompt = jnp.zeros((x_prompt.shape[0], HGRN_HEADS, HGRN_KEY_DIM, HGRN_VAL_DIM), state_hgrn.dtype)
    for l in range(DEPTH):
        lw = (w_ada[l], b_ada[l], norm_mix_w[l], w_in[l], hgrn_gnorm_w[l], w_up_attn[l], w_up_hgrn[l], w_out[l], norm_ffn_w[l], w_ff1[l], w_ff2[l])
        attn_prompt = lambda q, k, v: moba_prompt(q, k, v, rel_bias)
        attn_sample = lambda q, k, v, l=l: moba_sample(q, k, v, cache_k[l], cache_v[l], page_table, rel_bias)
        xp, kp, vp, sp = trunk_layer(xp, c_prompt, attn_prompt, s0_prompt, lb_all[l], *lw)
        xs, ksm, vsm, ssm = trunk_layer(xs, c_sample, attn_sample, state_hgrn[l], lb_all[l], *lw)
        kp_list.append(kp); vp_list.append(vp); sp_list.append(sp)
        ks_list.append(ksm); vs_list.append(vsm); ss_list.append(ssm)
    y_prompt = rmsnorm(xp, final_norm_w)
    y_sample = rmsnorm(xs, final_norm_w)
    k_prompt = jnp.stack(kp_list, axis=0)
    v_prompt = jnp.stack(vp_list, axis=0)
    state_prompt = jnp.stack(sp_list, axis=0)
    k_sample = jnp.stack(ks_list, axis=0)
    v_sample = jnp.stack(vs_list, axis=0)
    state_sample = jnp.stack(ss_list, axis=0)
    return (y_prompt, y_sample, k_prompt, v_prompt, state_prompt, k_sample, v_sample, state_sample)
```

```python
import functools
import math

import numpy as np
import jax
import jax.numpy as jnp
from jax import lax
from jax.experimental import pallas as pl
from jax.experimental.pallas import tpu as pltpu

F32, BF16, I32 = jnp.float32, jnp.bfloat16, jnp.int32
HIGHEST = lax.Precision.HIGHEST

ATTN_HEADS = 8
ATTN_HEAD_DIM = 64
ATTN_WIDTH = ATTN_HEADS * ATTN_HEAD_DIM
MOBA_BLOCK = 256
MOBA_TOPK = 3
PAGE_SIZE = 128
N_BUCKETS = 32
MAX_DISTANCE = 128
HGRN_HEADS = 4
HGRN_DIM = 128
HGRN_WIDTH = HGRN_HEADS * HGRN_DIM
N_ADA = 6
EPS = 1e-6
SCALE = ATTN_HEAD_DIM ** -0.5

LANES = 128
SUBLANES = 8
VMEM_BYTES_V7X = 64 * 1024 * 1024
VMEM_LIMIT = VMEM_BYTES_V7X * 7 // 8

NEG = -1e30
SUB = 16
ROW_TILE = 256
PAGES_PER_STEP = 8
PAIR = 2 * ATTN_HEAD_DIM
PAIR_SLOTS = 8


def _params(n_grid):
    return pltpu.CompilerParams(dimension_semantics=("arbitrary",) * n_grid, vmem_limit_bytes=VMEM_LIMIT)


def _const_spec(shape):
    nd = len(shape)
    return pl.BlockSpec(shape, lambda *_: (0,) * nd, pipeline_mode=pl.Buffered(1))


def _sigmoid(x):
    return jax.nn.sigmoid(x)


def _rms_mod(x, w, shift, scale):
    var = jnp.mean(x * x, axis=-1, keepdims=True)
    return (x * lax.rsqrt(var + EPS) * w) * (1.0 + scale) + shift


def _ada_kernel(c_ref, w_ref, b_ref, o_ref):
    c = c_ref[...]
    o_ref[...] = jnp.dot(c * _sigmoid(c), w_ref[...], preferred_element_type=F32, precision=HIGHEST) + b_ref[...]


def _ada(c_all, w_ada, b_ada):
    n, d = c_all.shape
    width = w_ada.shape[1]
    tn = 1536
    assert width % tn == 0
    return pl.pallas_call(
        _ada_kernel,
        grid=(width // tn,),
        in_specs=[pl.BlockSpec((n, d), lambda j: (0, 0)),
                  pl.BlockSpec((d, tn), lambda j: (0, j)),
                  pl.BlockSpec((1, tn), lambda j: (0, j))],
        out_specs=pl.BlockSpec((n, tn), lambda j: (0, j)),
        out_shape=jax.ShapeDtypeStruct((n, width), F32),
        compiler_params=_params(1),
        name="ada",
    )(c_all, w_ada, b_ada)


def _bucket_np(dist):
    n = np.maximum(dist, 0)
    max_exact = N_BUCKETS // 2
    nf = np.maximum(n, max_exact).astype(np.float32)
    large = max_exact + (np.log(nf / np.float32(max_exact)) / np.float32(math.log(MAX_DISTANCE / max_exact))
                         * np.float32(N_BUCKETS - max_exact)).astype(np.int32)
    large = np.minimum(large, N_BUCKETS - 1)
    return np.where(n < max_exact, n, large).astype(np.int32)


def _bucket_tables():
    i = np.arange(MOBA_BLOCK)[:, None]
    j = np.arange(MOBA_BLOCK)[None, :]
    own = np.where(j <= i, _bucket_np(i - j), -1).astype(np.int32)
    prev = _bucket_np(MOBA_BLOCK + i - j)
    assert int(_bucket_np(np.array([MOBA_BLOCK + 1]))[0]) == N_BUCKETS - 1
    samp = np.full((1, MOBA_BLOCK + LANES), N_BUCKETS - 1, np.int32)
    samp[0, :MOBA_BLOCK] = _bucket_np(MOBA_BLOCK - np.arange(MOBA_BLOCK))
    samp[0, MOBA_BLOCK] = 0
    return own, prev, samp


def _bias_kernel(rb_ref, bo_ref, bp_ref, bs_ref, to_ref, tp_ref, ts_ref):
    h = pl.program_id(0)
    far = rb_ref[N_BUCKETS - 1, h]

    def table(bucket):
        acc = jnp.where(bucket < 0, NEG, 0.0).astype(F32)
        for k in range(N_BUCKETS - 1):
            acc = jnp.where(bucket == k, rb_ref[k, h] - far, acc)
        return acc

    to_ref[0] = table(bo_ref[...])
    tp_ref[0] = table(bp_ref[...])
    ts_ref[0] = table(bs_ref[...])


def _bias_tables(rel_bias):
    own, prev, samp = _bucket_tables()
    nh = rel_bias.shape[1]
    b = MOBA_BLOCK
    return pl.pallas_call(
        _bias_kernel,
        grid=(nh,),
        in_specs=[pl.BlockSpec(memory_space=pltpu.SMEM),
                  pl.BlockSpec((b, b), lambda h: (0, 0)),
                  pl.BlockSpec((b, b), lambda h: (0, 0)),
                  pl.BlockSpec((1, b + LANES), lambda h: (0, 0))],
        out_specs=[pl.BlockSpec((1, b, b), lambda h: (h, 0, 0)),
                   pl.BlockSpec((1, b, b), lambda h: (h, 0, 0)),
                   pl.BlockSpec((1, 1, b + LANES), lambda h: (h, 0, 0))],
        out_shape=[jax.ShapeDtypeStruct((nh, b, b), F32),
                   jax.ShapeDtypeStruct((nh, b, b), F32),
                   jax.ShapeDtypeStruct((nh, 1, b + LANES), F32)],
        compiler_params=_params(1),
        name="bias_tables",
    )(rel_bias, jnp.asarray(own), jnp.asarray(prev), jnp.asarray(samp))


def _inproj_kernel(n_t, x_ref, ada_ref, nw_ref, wq_ref, wh_ref, wg_ref, wt_ref, qa_ref, hg_ref, gt_ref, *t_refs):
    d = x_ref.shape[-1]
    ada = ada_ref[0]
    h = _rms_mod(x_ref[0], nw_ref[...], ada[:, 0:d], ada[:, d:2 * d]).astype(BF16)
    qa_ref[0] = jnp.dot(h, wq_ref[...], preferred_element_type=F32)
    hg_ref[0] = jnp.dot(h, wh_ref[...], preferred_element_type=F32)
    gt_ref[0] = jnp.dot(h, wg_ref[...], preferred_element_type=F32)
    t = lax.dot_general(wt_ref[...], h, (((1,), (1,)), ((), ())), preferred_element_type=F32)
    for i in range(n_t):
        ref = t_refs[i]
        ref[0] = t[i * ATTN_WIDTH:(i + 1) * ATTN_WIDTH].reshape(ref.shape[1:])


def _inproj(x, ada, nw, wq, wh, wg, wt, tm):
    b, s, d = x.shape
    r = ada.shape[1]
    n_t = wt.shape[0] // ATTN_WIDTH
    assert s % tm == 0 and r in (1, s) and (r == 1 or tm == s)
    t_shapes, t_specs = [], []
    for i in range(n_t):
        if i < 2:
            t_shapes.append(jax.ShapeDtypeStruct((b, ATTN_HEADS, ATTN_HEAD_DIM, s), F32))
            t_specs.append(pl.BlockSpec((1, ATTN_HEADS, ATTN_HEAD_DIM, tm), lambda i_, j_: (i_, 0, 0, j_)))
        else:
            t_shapes.append(jax.ShapeDtypeStruct((b, ATTN_WIDTH, s), F32))
            t_specs.append(pl.BlockSpec((1, ATTN_WIDTH, tm), lambda i_, j_: (i_, 0, j_)))
    row = lambda w: pl.BlockSpec((1, tm, w), lambda i_, j_: (i_, j_, 0))
    return pl.pallas_call(
        functools.partial(_inproj_kernel, n_t),
        grid=(b, s // tm),
        in_specs=[row(d),
                  pl.BlockSpec((1, r if r == 1 else tm, N_ADA * d), lambda i_, j_: (i_, 0, 0)),
                  _const_spec(nw.shape), _const_spec(wq.shape), _const_spec(wh.shape), _const_spec(wg.shape),
                  _const_spec(wt.shape)],
        out_specs=[row(wq.shape[1]), row(wh.shape[1]), row(wg.shape[1])] + t_specs,
        out_shape=[jax.ShapeDtypeStruct((b, s, wq.shape[1]), F32),
                   jax.ShapeDtypeStruct((b, s, wh.shape[1]), F32),
                   jax.ShapeDtypeStruct((b, s, wg.shape[1]), F32)] + t_shapes,
        compiler_params=_params(2),
        name="inproj",
    )(x, ada, nw, wq, wh, wg, wt)


def _attn_consts(s):
    nb = s // MOBA_BLOCK
    key_blk = np.arange(s) // MOBA_BLOCK
    avg = np.zeros((s, LANES), np.float32)
    avg[np.arange(s), key_blk] = 1.0 / MOBA_BLOCK
    pt = np.zeros((LANES, LANES), np.float32)
    gm = np.zeros((LANES, LANES), np.float32)
    for b in range(nb):
        for b2 in range(nb):
            pt[b2, b * PAIR_SLOTS + b2] += 1.0
            pt[b, b * PAIR_SLOTS + b2] -= 1.0
            gm[b * PAIR_SLOTS + b2, b] = 1.0
    et = np.zeros((LANES, s), np.float32)
    et[key_blk, np.arange(s)] = 1.0
    return jnp.asarray(avg), jnp.asarray(pt), jnp.asarray(gm, BF16), jnp.asarray(et, BF16)


def _attn_prompt_kernel(q_ref, kt_ref, vt_ref, to_ref, tp_ref, avg_ref, pt_ref, gm_ref, et_ref, o_ref, kaug_ref, vtb_ref):
    s_len = q_ref.shape[1]
    nb = s_len // MOBA_BLOCK
    blk = MOBA_BLOCK
    kt = kt_ref[0].reshape(PAIR, s_len)
    kaug_ref[0:PAIR, :] = kt.astype(BF16)
    kaug_ref[PAIR:2 * PAIR, :] = et_ref[...]
    vtb_ref[...] = vt_ref[0].reshape(PAIR, s_len).astype(BF16)
    lane = lax.broadcasted_iota(I32, (2 * blk, LANES), 1)
    row = lax.broadcasted_iota(I32, (2 * blk, LANES), 0)
    own_head = (lane >= ATTN_HEAD_DIM) == (row >= blk)
    first_head = lax.broadcasted_iota(I32, (blk, LANES), 1) < ATTN_HEAD_DIM
    if nb > MOBA_TOPK + 1:
        means = jnp.dot(kt, avg_ref[...], preferred_element_type=F32, precision=HIGHEST)
        dmean = jnp.dot(means, pt_ref[...], preferred_element_type=F32, precision=HIGHEST)
        blk_b = lane // PAIR_SLOTS
        blk_b2 = lane % PAIR_SLOTS
    t_own = to_ref[...].reshape(2 * blk, blk)
    t_prev = tp_ref[...].reshape(2 * blk, blk)
    for j in range(nb):
        n_keys = (j + 1) * blk
        q2 = q_ref[0, j * blk:(j + 1) * blk, :] * SCALE
        qs = jnp.where(own_head, jnp.concatenate([q2, q2], axis=0), 0.0)
        if j > MOBA_TOPK:
            diff = jnp.dot(qs, dmean, preferred_element_type=F32, precision=HIGHEST)
            beats = ((diff > 0.0) | ((diff == 0.0) & (blk_b2 < blk_b))) & (blk_b2 < j) & (blk_b < j)
            rank = jnp.dot(jnp.where(beats, 1.0, 0.0).astype(BF16), gm_ref[...], preferred_element_type=F32)
            keep = (lane == j) | ((lane < j) & (rank < MOBA_TOPK)) | (lane >= PAIR_SLOTS)
            qaug = jnp.concatenate([qs.astype(BF16), jnp.where(keep, 0.0, NEG).astype(BF16)], axis=1)
            sc = jnp.dot(qaug, kaug_ref[:, 0:n_keys], preferred_element_type=F32)
        else:
            sc = jnp.dot(qs.astype(BF16), kaug_ref[0:PAIR, 0:n_keys], preferred_element_type=F32)
        pieces = []
        if j >= 2:
            pieces.append(sc[:, :n_keys - 2 * blk])
        if j >= 1:
            pieces.append(sc[:, n_keys - 2 * blk:n_keys - blk] + t_prev)
        pieces.append(sc[:, n_keys - blk:] + t_own)
        sc = jnp.concatenate(pieces, axis=1) if len(pieces) > 1 else pieces[0]
        m = jnp.max(sc, axis=1, keepdims=True)
        p = jnp.exp(sc - m)
        den = jnp.sum(p, axis=1, keepdims=True)
        o2 = lax.dot_general(p.astype(BF16), vtb_ref[:, 0:n_keys], (((1,), (1,)), ((), ())), preferred_element_type=F32)
        o2 = o2 / den
        o_ref[0, j * blk:(j + 1) * blk, :] = jnp.where(first_head, o2[:blk], o2[blk:])


def _attn_prompt(qa, kt, vt, t_own, t_prev):
    b, s, _ = qa.shape
    assert s % MOBA_BLOCK == 0 and s // MOBA_BLOCK <= PAIR_SLOTS
    avg, pt, gm, et = _attn_consts(s)
    n_pairs = ATTN_HEADS // 2
    return pl.pallas_call(
        _attn_prompt_kernel,
        grid=(b, n_pairs),
        in_specs=[pl.BlockSpec((1, s, PAIR), lambda i, p: (i, 0, p)),
                  pl.BlockSpec((1, 2, ATTN_HEAD_DIM, s), lambda i, p: (i, p, 0, 0)),
                  pl.BlockSpec((1, 2, ATTN_HEAD_DIM, s), lambda i, p: (i, p, 0, 0)),
                  pl.BlockSpec((2, MOBA_BLOCK, MOBA_BLOCK), lambda i, p: (p, 0, 0)),
                  pl.BlockSpec((2, MOBA_BLOCK, MOBA_BLOCK), lambda i, p: (p, 0, 0)),
                  _const_spec(avg.shape), _const_spec(pt.shape), _const_spec(gm.shape), _const_spec(et.shape)],
        out_specs=pl.BlockSpec((1, s, PAIR), lambda i, p: (i, 0, p)),
        out_shape=jax.ShapeDtypeStruct((b, s, ATTN_WIDTH), F32),
        scratch_shapes=[pltpu.VMEM((2 * PAIR, s), BF16), pltpu.VMEM((PAIR, s), BF16)],
        compiler_params=_params(2),
        name="attn_prompt",
    )(qa, kt, vt, t_own, t_prev, avg, pt, gm, et)


def _lower_bound(logits, axis):
    m = jnp.max(logits, axis=axis, keepdims=True)
    e = jnp.exp(logits - m)
    first = e[0:1] if axis == 0 else e[:, 0:1]
    return first / jnp.sum(e, axis=axis, keepdims=True)


def _gated_out(o, gw, og):
    var = jnp.mean(o * o, axis=-1, keepdims=True)
    return (o * lax.rsqrt(var + EPS) * gw) * (og * _sigmoid(og))


def _hgrn_prompt_kernel(q_ref, f_ref, v_ref, og_ref, lbl_ref, gw_ref, tri_ref, o_ref, s_ref, qt_s, kh_s, od_s, dl_s):
    t_len = q_ref.shape[1]
    n_sub = LANES // SUB
    lb = _lower_bound(lbl_ref[...], 0)
    gw = gw_ref[...]
    tri = tri_ref[...]
    sub_row = lax.broadcasted_iota(I32, (n_sub, SUB, HGRN_DIM), 1)

    def pass_a(t, carry):
        r0 = pl.multiple_of(t * LANES, LANES)
        qr = q_ref[0, pl.ds(r0, LANES), :]
        q = qr * _sigmoid(qr)
        f = lb + (1.0 - lb) * _sigmoid(f_ref[0, pl.ds(r0, LANES), :])
        kk = 1.0 - f
        v = v_ref[0, pl.ds(r0, LANES), :]
        b = jnp.dot(tri, jnp.log(f), preferred_element_type=F32, precision=HIGHEST)
        b3 = b.reshape(n_sub, SUB, HGRN_DIM)
        b_end = b3[:, SUB - 1:SUB, :]
        qt_s[pl.ds(r0, LANES), :] = q * jnp.exp(b)
        k3 = kk.reshape(n_sub, SUB, HGRN_DIM)
        kh_s[pl.ds(r0, LANES), :] = (k3 * jnp.exp(b_end - b3)).reshape(LANES, HGRN_DIM)
        dl_s[pl.ds(pl.multiple_of(t * n_sub, n_sub), n_sub), :] = jnp.exp(b_end).reshape(n_sub, HGRN_DIM)
        q3 = q.reshape(n_sub, SUB, HGRN_DIM)
        v3 = v.reshape(n_sub, SUB, HGRN_DIM)
        od = jnp.zeros((n_sub, SUB, HGRN_DIM), F32)
        for s in range(SUB):
            e = jnp.exp(jnp.minimum(b3 - b3[:, s:s + 1, :], 0.0))
            w = jnp.sum(q3 * (k3[:, s:s + 1, :] * e), axis=-1, keepdims=True)
            od = od + jnp.where(sub_row >= s, w, 0.0) * v3[:, s:s + 1, :]
        od_s[pl.ds(r0, LANES), :] = od.reshape(LANES, HGRN_DIM)
        return carry

    lax.fori_loop(0, t_len // LANES, pass_a, 0)

    def pass_b(i, st):
        r0 = pl.multiple_of(i * SUB, SUB)
        qt = qt_s[pl.ds(r0, SUB), :].astype(BF16)
        kh = kh_s[pl.ds(r0, SUB), :].astype(BF16)
        v = v_ref[0, pl.ds(r0, SUB), :].astype(BF16)
        o = lax.dot_general(qt, st.astype(BF16), (((1,), (1,)), ((), ())), preferred_element_type=F32) + od_s[pl.ds(r0, SUB), :]
        o_ref[0, pl.ds(r0, SUB), :] = _gated_out(o, gw, og_ref[0, pl.ds(r0, SUB), :])
        upd = lax.dot_general(v, kh, (((0,), (0,)), ((), ())), preferred_element_type=F32)
        return st * dl_s[pl.ds(i, 1), :] + upd

    st = lax.fori_loop(0, t_len // SUB, pass_b, jnp.zeros((HGRN_DIM, HGRN_DIM), F32))
    s_ref[0, 0] = st.T


def _hgrn_tri():
    r = np.arange(LANES)
    return jnp.asarray(((r[:, None] // SUB == r[None, :] // SUB) & (r[None, :] <= r[:, None])).astype(np.float32))


def _hgrn_prompt(hg, lb_logits, gw):
    b, s, _ = hg.shape
    assert s % LANES == 0
    col = lambda part: pl.BlockSpec((1, s, HGRN_DIM), lambda i, h, part=part: (i, 0, part * HGRN_HEADS + h))
    return pl.pallas_call(
        _hgrn_prompt_kernel,
        grid=(b, HGRN_HEADS),
        in_specs=[col(0), col(1), col(2), col(3),
                  pl.BlockSpec((lb_logits.shape[0], HGRN_DIM), lambda i, h: (0, h)),
                  _const_spec(gw.shape), _const_spec((LANES, LANES))],
        out_specs=[pl.BlockSpec((1, s, HGRN_DIM), lambda i, h: (i, 0, h)),
                   pl.BlockSpec((1, 1, HGRN_DIM, HGRN_DIM), lambda i, h: (i, h, 0, 0))],
        out_shape=[jax.ShapeDtypeStruct((b, s, HGRN_WIDTH), F32),
                   jax.ShapeDtypeStruct((b, HGRN_HEADS, HGRN_DIM, HGRN_DIM), F32)],
        scratch_shapes=[pltpu.VMEM((s, HGRN_DIM), F32), pltpu.VMEM((s, HGRN_DIM), F32), pltpu.VMEM((s, HGRN_DIM), F32),
                        pltpu.VMEM((s // SUB, HGRN_DIM), F32)],
        compiler_params=_params(2),
        name="hgrn_prompt",
    )(hg, hg, hg, hg, lb_logits, gw, _hgrn_tri())


def _hgrn_sample_kernel(xt_ref, hg_ref, lblt_ref, gw_ref, st_ref, o_ref, so_ref):
    n = pl.program_id(0)
    width = HGRN_WIDTH
    xt = xt_ref[0]
    pick = lax.broadcasted_iota(I32, xt.shape, 1) == n
    cols = jnp.sum(jnp.where(pick, xt, 0.0), axis=1, keepdims=True)
    lb = _lower_bound(lblt_ref[...], 1)
    hg = hg_ref[0]
    for h in range(HGRN_HEADS):
        rows = slice(h * HGRN_DIM, (h + 1) * HGRN_DIM)
        qr = cols[rows]
        q = qr * _sigmoid(qr)
        lbh = lb[rows]
        f = lbh + (1.0 - lbh) * _sigmoid(cols[width + h * HGRN_DIM:width + (h + 1) * HGRN_DIM])
        v = hg[:, 2 * width + h * HGRN_DIM:2 * width + (h + 1) * HGRN_DIM]
        og = hg[:, 3 * width + h * HGRN_DIM:3 * width + (h + 1) * HGRN_DIM]
        s_new = f * st_ref[0, h] + (1.0 - f) * v
        so_ref[0, h] = s_new
        o = jnp.sum(q * s_new, axis=0, keepdims=True)
        o_ref[0, :, rows] = _gated_out(o, gw_ref[...], og)


def _hgrn_sample(xt, hg, lb_logits_t, gw, state):
    db = state.shape[0]
    return pl.pallas_call(
        _hgrn_sample_kernel,
        grid=(db,),
        in_specs=[_const_spec(xt.shape),
                  pl.BlockSpec((1, 1, hg.shape[2]), lambda n: (n, 0, 0)),
                  _const_spec(lb_logits_t.shape), _const_spec(gw.shape),
                  pl.BlockSpec((1,) + state.shape[1:], lambda n: (n, 0, 0, 0))],
        out_specs=[pl.BlockSpec((1, 1, HGRN_WIDTH), lambda n: (n, 0, 0)),
                   pl.BlockSpec((1,) + state.shape[1:], lambda n: (n, 0, 0, 0))],
        out_shape=[jax.ShapeDtypeStruct((db, 1, HGRN_WIDTH), F32),
                   jax.ShapeDtypeStruct(state.shape, F32)],
        compiler_params=_params(1),
        name="hgrn_sample",
    )(xt, hg, lb_logits_t, gw, state)


def _post_kernel(x_ref, oa_ref, oh_ref, gt_ref, ada_ref, wua_ref, wuh_ref, wo_ref, nf_ref, w1_ref, w2_ref, fn_ref, y_ref):
    d = x_ref.shape[-1]
    ada = ada_ref[0]
    g = gt_ref[0]
    ua = jnp.dot(oa_ref[0].astype(BF16), wua_ref[...], preferred_element_type=F32)
    uh = jnp.dot(oh_ref[0].astype(BF16), wuh_ref[...], preferred_element_type=F32)
    merged = _sigmoid(g[:, :d]) * ua + _sigmoid(g[:, d:]) * uh
    x1 = x_ref[0] + ada[:, 2 * d:3 * d] * jnp.dot(merged.astype(BF16), wo_ref[...], preferred_element_type=F32)
    h2 = _rms_mod(x1, nf_ref[...], ada[:, 3 * d:4 * d], ada[:, 4 * d:5 * d]).astype(BF16)
    a = jnp.maximum(jnp.dot(h2, w1_ref[...], preferred_element_type=F32), 0.0)
    x2 = x1 + ada[:, 5 * d:6 * d] * jnp.dot((a * a).astype(BF16), w2_ref[...], preferred_element_type=F32)
    var = jnp.mean(x2 * x2, axis=-1, keepdims=True)
    y_ref[0] = x2 * lax.rsqrt(var + EPS) * fn_ref[...]


def _post(x, oa, oh, gt, ada, wua, wuh, wo, nf, w1, w2, fn, tm):
    b, s, d = x.shape
    r = ada.shape[1]
    assert s % tm == 0 and r in (1, s) and (r == 1 or tm == s)
    row = lambda w: pl.BlockSpec((1, tm, w), lambda i, j: (i, j, 0))
    return pl.pallas_call(
        _post_kernel,
        grid=(b, s // tm),
        in_specs=[row(d), row(oa.shape[2]), row(oh.shape[2]), row(gt.shape[2]),
                  pl.BlockSpec((1, r if r == 1 else tm, N_ADA * d), lambda i, j: (i, 0, 0)),
                  _const_spec(wua.shape), _const_spec(wuh.shape), _const_spec(wo.shape), _const_spec(nf.shape),
                  _const_spec(w1.shape), _const_spec(w2.shape), _const_spec(fn.shape)],
        out_specs=row(d),
        out_shape=jax.ShapeDtypeStruct((b, s, d), F32),
        compiler_params=_params(2),
        name="post",
    )(x, oa, oh, gt, ada, wua, wuh, wo, nf, w1, w2, fn)


def _kpass_kernel(pt_ref, qt_ref, *refs):
    k_refs, sc_ref, qb_ref = refs[:PAGES_PER_STEP], refs[PAGES_PER_STEP], refs[PAGES_PER_STEP + 1]
    n = pl.program_id(0)

    @pl.when(pl.program_id(1) == 0)
    def _():
        qt = qt_ref[0]
        pick = lax.broadcasted_iota(I32, qt.shape, 1) == n
        qb_ref[...] = jnp.broadcast_to(jnp.sum(jnp.where(pick, qt, 0.0), axis=1, keepdims=True), qb_ref.shape)

    qb = qb_ref[...]
    for j in range(PAGES_PER_STEP):
        kt = k_refs[j][0].reshape(ATTN_WIDTH, PAGE_SIZE)
        sc_ref[0, :, j * PAGE_SIZE:(j + 1) * PAGE_SIZE] = jnp.sum(
            (kt * qb).reshape(ATTN_HEADS, ATTN_HEAD_DIM, PAGE_SIZE), axis=1)


def _kpass(page_table, qt, cache_kt):
    db, n_pages = page_table.shape
    assert n_pages % PAGES_PER_STEP == 0
    page_spec = lambda j: pl.BlockSpec((1, ATTN_HEADS, ATTN_HEAD_DIM, PAGE_SIZE),
                                       lambda n, g, pt, j=j: (pt[n, g * PAGES_PER_STEP + j], 0, 0, 0))
    return pl.pallas_call(
        _kpass_kernel,
        grid_spec=pltpu.PrefetchScalarGridSpec(
            num_scalar_prefetch=1,
            grid=(db, n_pages // PAGES_PER_STEP),
            in_specs=[pl.BlockSpec(qt.shape, lambda n, g, pt: (0, 0, 0))] + [page_spec(j) for j in range(PAGES_PER_STEP)],
            out_specs=pl.BlockSpec((1, ATTN_HEADS, PAGES_PER_STEP * PAGE_SIZE), lambda n, g, pt: (n, 0, g)),
            scratch_shapes=[pltpu.VMEM((ATTN_WIDTH, PAGE_SIZE), F32)]),
        out_shape=jax.ShapeDtypeStruct((db, ATTN_HEADS, n_pages * PAGE_SIZE), F32),
        compiler_params=_params(2),
        name="sample_kpass",
    )(page_table, qt, *([cache_kt] * PAGES_PER_STEP))


def _select_kernel(sc_ref, qt_ref, kt_ref, ts_ref, ps_ref, idx_ref, po_ref):
    n = pl.program_id(0)
    sc = sc_ref[0]
    n_blk = sc.shape[1] // MOBA_BLOCK
    lane = lax.broadcasted_iota(I32, (ATTN_HEADS, LANES), 1).astype(F32)
    bs = jnp.full((ATTN_HEADS, LANES), -jnp.inf, F32)
    for b in range(n_blk):
        tot = jnp.sum(sc[:, b * MOBA_BLOCK:(b + 1) * MOBA_BLOCK], axis=1, keepdims=True)
        bs = jnp.where(lane == float(b), tot * (1.0 / MOBA_BLOCK), bs)
    picks = []
    for _ in range(MOBA_TOPK):
        m = jnp.max(bs, axis=1, keepdims=True)
        i = jnp.min(jnp.where(bs == m, lane, float(LANES)), axis=1, keepdims=True)
        picks.append(i)
        bs = jnp.where(lane == i, -jnp.inf, bs)
    ts = ts_ref[...].reshape(ATTN_HEADS, MOBA_BLOCK + LANES)
    logits = []
    for i in picks:
        blk = jnp.zeros((ATTN_HEADS, MOBA_BLOCK), F32)
        for b in range(n_blk):
            blk = jnp.where(i == float(b), sc[:, b * MOBA_BLOCK:(b + 1) * MOBA_BLOCK], blk)
        logits.append(blk * SCALE + jnp.where(i == float(n_blk - 1), ts[:, :MOBA_BLOCK], 0.0))
    lg = jnp.concatenate(logits, axis=1)
    prod = qt_ref[0] * kt_ref[0].reshape(ATTN_WIDTH, -1)
    own_all = jnp.sum(prod.reshape(ATTN_HEADS, ATTN_HEAD_DIM, prod.shape[1]), axis=1)
    pick = lax.broadcasted_iota(I32, own_all.shape, 1) == n
    own = jnp.sum(jnp.where(pick, own_all, 0.0), axis=1, keepdims=True) * SCALE + ts[:, MOBA_BLOCK:MOBA_BLOCK + 1]
    m = jnp.maximum(jnp.max(lg, axis=1, keepdims=True), own)
    e = jnp.exp(lg - m)
    eo = jnp.exp(own - m)
    den = jnp.sum(e, axis=1, keepdims=True) + eo
    ps_ref[0] = e / den
    po_ref[0] = jnp.broadcast_to(eo / den, (ATTN_HEADS, LANES))
    idx = jnp.zeros((ATTN_HEADS, LANES), F32)
    for t, i in enumerate(picks):
        idx = jnp.where(lane == float(t), i, idx)
    idx_ref[0] = idx.astype(I32)


def _select(scores, qt, kt_new, ts):
    db, nh, past = scores.shape
    assert past % MOBA_BLOCK == 0 and MOBA_TOPK <= past // MOBA_BLOCK <= LANES
    wsel = MOBA_TOPK * MOBA_BLOCK
    per = lambda w: pl.BlockSpec((1, nh, w), lambda n: (n, 0, 0))
    return pl.pallas_call(
        _select_kernel,
        grid=(db,),
        in_specs=[per(past), _const_spec(qt.shape), _const_spec(kt_new.shape), _const_spec(ts.shape)],
        out_specs=[per(wsel), per(LANES), per(LANES)],
        out_shape=[jax.ShapeDtypeStruct((db, nh, wsel), F32),
                   jax.ShapeDtypeStruct((db, nh, LANES), I32),
                   jax.ShapeDtypeStruct((db, nh, LANES), F32)],
        compiler_params=_params(1),
        name="sample_select",
    )(scores, qt, kt_new, ts)


N_SEL_PAGES = MOBA_TOPK * (MOBA_BLOCK // PAGE_SIZE)


def _vpass_kernel(pp_ref, ps_ref, po_ref, vn_ref, *refs):
    v_refs, o_ref = refs[:N_SEL_PAGES], refs[N_SEL_PAGES]
    ps = ps_ref[0, 0]
    acc = jnp.zeros((SUBLANES, ATTN_HEAD_DIM), F32)
    for j in range(N_SEL_PAGES):
        pr = jnp.broadcast_to(ps[:, j * PAGE_SIZE:(j + 1) * PAGE_SIZE], (SUBLANES, PAGE_SIZE))
        acc = acc + lax.dot_general(pr, v_refs[j][0, 0], (((1,), (1,)), ((), ())), preferred_element_type=F32, precision=HIGHEST)
    o_ref[0, 0] = acc[0:1] + po_ref[0, 0][:, 0:ATTN_HEAD_DIM] * vn_ref[0, 0]


def _vpass(phys, p_sel, p_own, v_new, cache_vt):
    db, nh = p_sel.shape[:2]
    page_spec = lambda j: pl.BlockSpec((1, 1, ATTN_HEAD_DIM, PAGE_SIZE),
                                       lambda n, h, pp, j=j: (pp[(n * nh + h) * N_SEL_PAGES + j], h, 0, 0))
    per = lambda w: pl.BlockSpec((1, 1, 1, w), lambda n, h, pp: (n, h, 0, 0))
    return pl.pallas_call(
        _vpass_kernel,
        grid_spec=pltpu.PrefetchScalarGridSpec(
            num_scalar_prefetch=1,
            grid=(db, nh),
            in_specs=[per(p_sel.shape[3]), per(LANES), per(ATTN_HEAD_DIM)] + [page_spec(j) for j in range(N_SEL_PAGES)],
            out_specs=per(ATTN_HEAD_DIM)),
        out_shape=jax.ShapeDtypeStruct((db, nh, 1, ATTN_HEAD_DIM), F32),
        compiler_params=_params(2),
        name="sample_vpass",
    )(phys, p_sel, p_own, v_new, *([cache_vt] * N_SEL_PAGES))


def kernel(x_prompt, x_sample, cache_k, cache_v, state_hgrn, page_table, c_prompt, c_sample, rel_bias, hgrn_lb_logits,
           w_ada, b_ada, norm_mix_w, w_in, hgrn_gnorm_w, w_up_attn, w_up_hgrn, w_out, norm_ffn_w, w_ff1, w_ff2, final_norm_w):
    depth = w_in.shape[0]
    assert depth == 1, "one-layer trunk"
    b, s, d = x_prompt.shape
    db, ds, _ = x_sample.shape
    assert ds == 1
    past = page_table.shape[1] * PAGE_SIZE

    w = w_in[0]
    a0, a1, a2 = ATTN_WIDTH, 2 * ATTN_WIDTH, 3 * ATTN_WIDTH
    h0 = a2 + 4 * HGRN_WIDTH
    wq = w[:, :a0].astype(BF16)
    wh = w[:, a2:h0].astype(BF16)
    wg = w[:, h0:].astype(BF16)
    wt_kv = w[:, a0:a2].T.astype(BF16)
    wt_s = jnp.concatenate([w[:, a0:a2], w[:, :a0], w[:, a2:a2 + 2 * HGRN_WIDTH]], axis=1).T.astype(BF16)
    wua, wuh, wo = w_up_attn[0].astype(BF16), w_up_hgrn[0].astype(BF16), w_out[0].astype(BF16)
    w1, w2 = w_ff1[0].astype(BF16), w_ff2[0].astype(BF16)
    fn = final_norm_w.reshape(1, d)

    ada = _ada(jnp.concatenate([c_prompt, c_sample], axis=0), w_ada[0], b_ada)
    ada_p = ada[:b].reshape(b, 1, N_ADA * d)
    ada_s = ada[b:].reshape(1, db, N_ADA * d)
    t_own, t_prev, t_samp = _bias_tables(rel_bias)

    qa, hg, gt, kt, vt = _inproj(x_prompt, ada_p, norm_mix_w, wq, wh, wg, wt_kv, ROW_TILE)
    o_attn = _attn_prompt(qa, kt, vt, t_own, t_prev)
    o_hgrn, state_p = _hgrn_prompt(hg, hgrn_lb_logits, hgrn_gnorm_w)
    y_prompt = _post(x_prompt, o_attn, o_hgrn, gt, ada_p, wua, wuh, wo, norm_ffn_w, w1, w2, fn, ROW_TILE)

    xs = x_sample.reshape(1, db, d)
    qa_s, hg_s, gt_s, kt_s, vt_s, qt_s, qht_s, fht_s = _inproj(xs, ada_s, norm_mix_w, wq, wh, wg, wt_s, db)
    cache_kt = jnp.transpose(cache_k[0], (0, 2, 3, 1))
    cache_vt = jnp.transpose(cache_v[0], (0, 2, 3, 1))
    scores = _kpass(page_table, qt_s, cache_kt)
    p_sel, idx, p_own = _select(scores, qt_s, kt_s, t_samp)
    ppb = MOBA_BLOCK // PAGE_SIZE
    logical = idx[:, :, :MOBA_TOPK, None] * ppb + jnp.arange(ppb, dtype=I32)
    phys = jnp.take_along_axis(page_table, logical.reshape(db, -1), axis=1).reshape(-1)
    v_new = jnp.transpose(vt_s[0], (2, 0, 1))[:, :, None, :]
    o_attn_s = _vpass(phys, p_sel[:, :, None, :], p_own[:, :, None, :], v_new, cache_vt).reshape(1, db, ATTN_WIDTH)
    xt = jnp.concatenate([qht_s, fht_s], axis=1)
    o_hgrn_s, state_s = _hgrn_sample(xt, hg_s.reshape(db, 1, -1), hgrn_lb_logits.T, hgrn_gnorm_w, state_hgrn[0])
    y_sample = _post(xs, o_attn_s, o_hgrn_s.reshape(1, db, HGRN_WIDTH), gt_s, ada_s, wua, wuh, wo, norm_ffn_w, w1, w2, fn, db)

    to_rows = lambda t: jnp.transpose(t, (0, 3, 1, 2))[None]
    return (y_prompt, y_sample.reshape(db, 1, d), to_rows(kt), to_rows(vt), state_p[None],
            to_rows(kt_s).reshape(1, db, 1, ATTN_HEADS, ATTN_HEAD_DIM), to_rows(vt_s).reshape(1, db, 1, ATTN_HEADS, ATTN_HEAD_DIM),
            state_s[None])
```

```python
import functools
import math

import numpy as np
import jax
import jax.numpy as jnp
from jax import lax
from jax.experimental import pallas as pl
from jax.experimental.pallas import tpu as pltpu

F32, BF16, I32 = jnp.float32, jnp.bfloat16, jnp.int32
HIGHEST = lax.Precision.HIGHEST

ATTN_HEADS = 8
ATTN_HEAD_DIM = 64
ATTN_WIDTH = ATTN_HEADS * ATTN_HEAD_DIM
MOBA_BLOCK = 256
MOBA_TOPK = 3
PAGE_SIZE = 128
N_BUCKETS = 32
MAX_DISTANCE = 128
HGRN_HEADS = 4
HGRN_DIM = 128
HGRN_WIDTH = HGRN_HEADS * HGRN_DIM
N_ADA = 6
EPS = 1e-6
SCALE = ATTN_HEAD_DIM ** -0.5

LANES = 128
SUBLANES = 8
VMEM_BYTES_V7X = 64 * 1024 * 1024
VMEM_LIMIT = VMEM_BYTES_V7X * 7 // 8

NEG = -1e30
SUB = 16
TILE_UNROLL = 4
ROW_TILE = 256
PAGES_PER_STEP = 16
PAIR = 2 * ATTN_HEAD_DIM
PAIR_SLOTS = 8


def _params(n_grid):
    return pltpu.CompilerParams(dimension_semantics=("arbitrary",) * n_grid, vmem_limit_bytes=VMEM_LIMIT)


def _const_spec(shape):
    nd = len(shape)
    return pl.BlockSpec(shape, lambda *_: (0,) * nd, pipeline_mode=pl.Buffered(1))


def _sigmoid(x):
    return jax.nn.sigmoid(x)


def _rms_mod(x, w, shift, scale):
    var = jnp.mean(x * x, axis=-1, keepdims=True)
    return (x * lax.rsqrt(var + EPS) * w) * (1.0 + scale) + shift


def _ada_kernel(c_ref, w_ref, b_ref, o_ref):
    c = c_ref[...]
    o_ref[...] = jnp.dot(c * _sigmoid(c), w_ref[...], preferred_element_type=F32, precision=HIGHEST) + b_ref[...]


def _ada(c_all, w_ada, b_ada):
    n, d = c_all.shape
    width = w_ada.shape[1]
    tn = 1536
    assert width % tn == 0
    return pl.pallas_call(
        _ada_kernel,
        grid=(width // tn,),
        in_specs=[pl.BlockSpec((n, d), lambda j: (0, 0)),
                  pl.BlockSpec((d, tn), lambda j: (0, j)),
                  pl.BlockSpec((1, tn), lambda j: (0, j))],
        out_specs=pl.BlockSpec((n, tn), lambda j: (0, j)),
        out_shape=jax.ShapeDtypeStruct((n, width), F32),
        compiler_params=_params(1),
        name="ada",
    )(c_all, w_ada, b_ada)


def _bucket_np(dist):
    n = np.maximum(dist, 0)
    max_exact = N_BUCKETS // 2
    nf = np.maximum(n, max_exact).astype(np.float32)
    large = max_exact + (np.log(nf / np.float32(max_exact)) / np.float32(math.log(MAX_DISTANCE / max_exact))
                         * np.float32(N_BUCKETS - max_exact)).astype(np.int32)
    large = np.minimum(large, N_BUCKETS - 1)
    return np.where(n < max_exact, n, large).astype(np.int32)


def _bucket_tables():
    i = np.arange(MOBA_BLOCK)[:, None]
    j = np.arange(MOBA_BLOCK)[None, :]
    own = np.where(j <= i, _bucket_np(i - j), -1).astype(np.int32)
    prev = _bucket_np(MOBA_BLOCK + i - j)
    assert int(_bucket_np(np.array([MOBA_BLOCK + 1]))[0]) == N_BUCKETS - 1
    samp = np.full((1, MOBA_BLOCK + LANES), N_BUCKETS - 1, np.int32)
    samp[0, :MOBA_BLOCK] = _bucket_np(MOBA_BLOCK - np.arange(MOBA_BLOCK))
    samp[0, MOBA_BLOCK] = 0
    return own, prev, samp


def _bias_kernel(rb_ref, bo_ref, bp_ref, bs_ref, to_ref, tp_ref, ts_ref):
    h = pl.program_id(0)
    far = rb_ref[N_BUCKETS - 1, h]

    def table(bucket):
        acc = jnp.where(bucket < 0, NEG, 0.0).astype(F32)
        for k in range(N_BUCKETS - 1):
            acc = jnp.where(bucket == k, rb_ref[k, h] - far, acc)
        return acc

    to_ref[0] = table(bo_ref[...])
    tp_ref[0] = table(bp_ref[...])
    ts_ref[0] = table(bs_ref[...])


def _bias_tables(rel_bias):
    own, prev, samp = _bucket_tables()
    nh = rel_bias.shape[1]
    b = MOBA_BLOCK
    return pl.pallas_call(
        _bias_kernel,
        grid=(nh,),
        in_specs=[pl.BlockSpec(memory_space=pltpu.SMEM),
                  pl.BlockSpec((b, b), lambda h: (0, 0)),
                  pl.BlockSpec((b, b), lambda h: (0, 0)),
                  pl.BlockSpec((1, b + LANES), lambda h: (0, 0))],
        out_specs=[pl.BlockSpec((1, b, b), lambda h: (h, 0, 0)),
                   pl.BlockSpec((1, b, b), lambda h: (h, 0, 0)),
                   pl.BlockSpec((1, 1, b + LANES), lambda h: (h, 0, 0))],
        out_shape=[jax.ShapeDtypeStruct((nh, b, b), F32),
                   jax.ShapeDtypeStruct((nh, b, b), F32),
                   jax.ShapeDtypeStruct((nh, 1, b + LANES), F32)],
        compiler_params=_params(1),
        name="bias_tables",
    )(rel_bias, jnp.asarray(own), jnp.asarray(prev), jnp.asarray(samp))


def _inproj_kernel(n_t, x_ref, ada_ref, nw_ref, wq_ref, wh_ref, wg_ref, wt_ref, qa_ref, hg_ref, gt_ref, *t_refs):
    d = x_ref.shape[-1]
    ada = ada_ref[0]
    h = _rms_mod(x_ref[0], nw_ref[...], ada[:, 0:d], ada[:, d:2 * d]).astype(BF16)
    qa_ref[0] = jnp.dot(h, wq_ref[...], preferred_element_type=F32)
    hg_ref[0] = jnp.dot(h, wh_ref[...], preferred_element_type=F32)
    gt_ref[0] = jnp.dot(h, wg_ref[...], preferred_element_type=F32)
    t = lax.dot_general(wt_ref[...], h, (((1,), (1,)), ((), ())), preferred_element_type=F32)
    for i in range(n_t):
        ref = t_refs[i]
        ref[0] = t[i * ATTN_WIDTH:(i + 1) * ATTN_WIDTH].reshape(ref.shape[1:])


def _inproj(x, ada, nw, wq, wh, wg, wt, tm):
    b, s, d = x.shape
    r = ada.shape[1]
    n_t = wt.shape[0] // ATTN_WIDTH
    assert s % tm == 0 and r in (1, s) and (r == 1 or tm == s)
    t_shapes, t_specs = [], []
    for i in range(n_t):
        if i < 2:
            t_shapes.append(jax.ShapeDtypeStruct((b, ATTN_HEADS, ATTN_HEAD_DIM, s), F32))
            t_specs.append(pl.BlockSpec((1, ATTN_HEADS, ATTN_HEAD_DIM, tm), lambda i_, j_: (i_, 0, 0, j_)))
        else:
            t_shapes.append(jax.ShapeDtypeStruct((b, ATTN_WIDTH, s), F32))
            t_specs.append(pl.BlockSpec((1, ATTN_WIDTH, tm), lambda i_, j_: (i_, 0, j_)))
    row = lambda w: pl.BlockSpec((1, tm, w), lambda i_, j_: (i_, j_, 0))
    return pl.pallas_call(
        functools.partial(_inproj_kernel, n_t),
        grid=(b, s // tm),
        in_specs=[row(d),
                  pl.BlockSpec((1, r if r == 1 else tm, N_ADA * d), lambda i_, j_: (i_, 0, 0)),
                  _const_spec(nw.shape), _const_spec(wq.shape), _const_spec(wh.shape), _const_spec(wg.shape),
                  _const_spec(wt.shape)],
        out_specs=[row(wq.shape[1]), row(wh.shape[1]), row(wg.shape[1])] + t_specs,
        out_shape=[jax.ShapeDtypeStruct((b, s, wq.shape[1]), F32),
                   jax.ShapeDtypeStruct((b, s, wh.shape[1]), F32),
                   jax.ShapeDtypeStruct((b, s, wg.shape[1]), F32)] + t_shapes,
        compiler_params=_params(2),
        name="inproj",
    )(x, ada, nw, wq, wh, wg, wt)


def _attn_consts(s):
    nb = s // MOBA_BLOCK
    key_blk = np.arange(s) // MOBA_BLOCK
    gm = np.zeros((LANES, LANES), np.float32)
    for b in range(nb):
        for b2 in range(nb):
            gm[b * PAIR_SLOTS + b2, b] = 1.0
    et = np.zeros((LANES, s), np.float32)
    et[key_blk, np.arange(s)] = 1.0
    return jnp.asarray(gm, BF16), jnp.asarray(et, BF16)


def _attn_prompt_kernel(q_ref, kt_ref, vt_ref, to_ref, tp_ref, gm_ref, et_ref, o_ref, kaug_ref, vtb_ref):
    s_len = q_ref.shape[1]
    nb = s_len // MOBA_BLOCK
    blk = MOBA_BLOCK
    kt = kt_ref[0].reshape(PAIR, s_len)
    kaug_ref[0:PAIR, :] = kt.astype(BF16)
    kaug_ref[PAIR:2 * PAIR, :] = et_ref[...]
    vtb_ref[...] = vt_ref[0].reshape(PAIR, s_len).astype(BF16)
    lane = lax.broadcasted_iota(I32, (2 * blk, LANES), 1)
    row = lax.broadcasted_iota(I32, (2 * blk, LANES), 0)
    own_head = (lane >= ATTN_HEAD_DIM) == (row >= blk)
    first_head = lax.broadcasted_iota(I32, (blk, LANES), 1) < ATTN_HEAD_DIM
    t_own = to_ref[...].reshape(2 * blk, blk)
    t_prev = tp_ref[...].reshape(2 * blk, blk)

    q_hi, q_lo = [], []
    for j in range(nb):
        q2 = q_ref[0, j * blk:(j + 1) * blk, :] * SCALE
        qs = jnp.where(own_head, jnp.concatenate([q2, q2], axis=0), 0.0)
        q_hi.append(qs.astype(BF16))
        q_lo.append((qs - q_hi[j].astype(F32)).astype(BF16) if j > MOBA_TOPK else None)

    mask_cols = [None] * nb
    if nb > MOBA_TOPK + 1:
        lane_k = lax.broadcasted_iota(I32, (PAIR, LANES), 1)
        dmean = jnp.zeros((PAIR, LANES), F32)
        for b in range(nb - 1):
            mb = jnp.sum(kt[:, b * blk:(b + 1) * blk], axis=1, keepdims=True) * (1.0 / blk)
            dmean = dmean + jnp.where(lane_k % PAIR_SLOTS == b, mb, 0.0) - jnp.where(lane_k // PAIR_SLOTS == b, mb, 0.0)
        d_hi = dmean.astype(BF16)
        d_lo = (dmean - d_hi.astype(F32)).astype(BF16)
        d_stack = jnp.concatenate([d_hi, d_hi, d_lo], axis=0)
        blk_b = lane // PAIR_SLOTS
        blk_b2 = lane % PAIR_SLOTS
        diffs = {j: jnp.dot(jnp.concatenate([q_hi[j], q_lo[j], q_hi[j]], axis=1), d_stack, preferred_element_type=F32)
                 for j in range(MOBA_TOPK + 1, nb)}
        ranks = {}
        for j, diff in diffs.items():
            beats = ((diff > 0.0) | ((diff == 0.0) & (blk_b2 < blk_b))) & (blk_b2 < j) & (blk_b < j)
            ranks[j] = jnp.dot(jnp.where(beats, 1.0, 0.0).astype(BF16), gm_ref[...], preferred_element_type=F32)
        for j, rank in ranks.items():
            keep = (lane == j) | ((lane < j) & (rank < MOBA_TOPK)) | (lane >= PAIR_SLOTS)
            mask_cols[j] = jnp.where(keep, 0.0, NEG).astype(BF16)

    def scores(j):
        n_keys = (j + 1) * blk
        if mask_cols[j] is None:
            return jnp.dot(q_hi[j], kaug_ref[0:PAIR, 0:n_keys], preferred_element_type=F32)
        return jnp.dot(jnp.concatenate([q_hi[j], mask_cols[j]], axis=1), kaug_ref[:, 0:n_keys], preferred_element_type=F32)

    sc_next = scores(0)
    for j in range(nb):
        n_keys = (j + 1) * blk
        sc = sc_next
        if j + 1 < nb:
            sc_next = scores(j + 1)
        pieces = []
        if j >= 2:
            pieces.append(sc[:, :n_keys - 2 * blk])
        if j >= 1:
            pieces.append(sc[:, n_keys - 2 * blk:n_keys - blk] + t_prev)
        pieces.append(sc[:, n_keys - blk:] + t_own)
        sc = jnp.concatenate(pieces, axis=1) if len(pieces) > 1 else pieces[0]
        m = jnp.max(sc, axis=1, keepdims=True)
        p = jnp.exp(sc - m)
        den = jnp.sum(p, axis=1, keepdims=True)
        o2 = lax.dot_general(p.astype(BF16), vtb_ref[:, 0:n_keys], (((1,), (1,)), ((), ())), preferred_element_type=F32)
        o2 = o2 / den
        o_ref[0, j * blk:(j + 1) * blk, :] = jnp.where(first_head, o2[:blk], o2[blk:])


def _attn_prompt(qa, kt, vt, t_own, t_prev):
    b, s, _ = qa.shape
    assert s % MOBA_BLOCK == 0 and s // MOBA_BLOCK <= PAIR_SLOTS
    gm, et = _attn_consts(s)
    n_pairs = ATTN_HEADS // 2
    return pl.pallas_call(
        _attn_prompt_kernel,
        grid=(b, n_pairs),
        in_specs=[pl.BlockSpec((1, s, PAIR), lambda i, p: (i, 0, p)),
                  pl.BlockSpec((1, 2, ATTN_HEAD_DIM, s), lambda i, p: (i, p, 0, 0)),
                  pl.BlockSpec((1, 2, ATTN_HEAD_DIM, s), lambda i, p: (i, p, 0, 0)),
                  pl.BlockSpec((2, MOBA_BLOCK, MOBA_BLOCK), lambda i, p: (p, 0, 0)),
                  pl.BlockSpec((2, MOBA_BLOCK, MOBA_BLOCK), lambda i, p: (p, 0, 0)),
                  _const_spec(gm.shape), _const_spec(et.shape)],
        out_specs=pl.BlockSpec((1, s, PAIR), lambda i, p: (i, 0, p)),
        out_shape=jax.ShapeDtypeStruct((b, s, ATTN_WIDTH), F32),
        scratch_shapes=[pltpu.VMEM((2 * PAIR, s), BF16), pltpu.VMEM((PAIR, s), BF16)],
        compiler_params=_params(2),
        name="attn_prompt",
    )(qa, kt, vt, t_own, t_prev, gm, et)


def _lower_bound(logits, axis):
    m = jnp.max(logits, axis=axis, keepdims=True)
    e = jnp.exp(logits - m)
    first = e[0:1] if axis == 0 else e[:, 0:1]
    return first / jnp.sum(e, axis=axis, keepdims=True)


def _gated_out(o, gw, og):
    var = jnp.mean(o * o, axis=-1, keepdims=True)
    return (o * lax.rsqrt(var + EPS) * gw) * (og * _sigmoid(og))


def _cumsum_rows(tri, x):
    hi = x.astype(BF16)
    rest = x - hi.astype(F32)
    mid = rest.astype(BF16)
    lo = (rest - mid.astype(F32)).astype(BF16)
    n = x.shape[1]
    y = jnp.dot(tri, jnp.concatenate([hi, mid, lo], axis=1), preferred_element_type=F32)
    return y[:, :n] + y[:, n:2 * n] + y[:, 2 * n:]


def _hgrn_prompt_kernel(q_ref, f_ref, v_ref, og_ref, lbl_ref, gw_ref, tri_ref, o_ref, s_ref, shift_s, vrow_s):
    t_len = q_ref.shape[1]
    n_sub = LANES // SUB
    half = SUB // 2
    dim = HGRN_DIM
    lb = _lower_bound(lbl_ref[...], 0)
    gw = gw_ref[...]
    tri = tri_ref[...]
    half_row = lax.broadcasted_iota(I32, (n_sub, half, dim), 1)
    trans_b = (((1,), (1,)), ((), ()))

    def prepare(t):
        rows = pl.ds(pl.multiple_of(t * LANES, LANES), LANES)
        qr = q_ref[0, rows, :]
        q = qr * _sigmoid(qr)
        f = lb + (1.0 - lb) * _sigmoid(f_ref[0, rows, :])
        v = v_ref[0, rows, :]
        cum = _cumsum_rows(tri, jnp.log2(f))
        return dict(rows=rows, q=q, kk=1.0 - f, v=v, vb=v.astype(BF16), cum=cum)

    def split(p):
        c3 = p["cum"].reshape(n_sub, SUB, dim)
        start = jnp.concatenate([jnp.zeros((1, 1, dim), F32), c3[:n_sub - 1, SUB - 1:SUB, :]], axis=0)
        p.update(c3=c3, start=start, b3=c3 - start, c_end=c3[n_sub - 1, SUB - 1:SUB, :],
                 q3=p["q"].reshape(n_sub, SUB, dim), k3=p["kk"].reshape(n_sub, SUB, dim))

    def first_products(p):
        p["upd"] = lax.dot_general(p["vb"], (p["kk"] * jnp.exp2(p["c_end"] - p["cum"])).astype(BF16),
                                   (((0,), (0,)), ((), ())), preferred_element_type=F32)
        qt3 = (p["q3"] * jnp.exp2(p["b3"])).astype(BF16)
        p["scores"] = []
        for i in range(1, n_sub):
            kt = (p["k3"][:i] * jnp.exp2(p["start"][i:i + 1] - p["c3"][:i])).reshape(i * SUB, dim).astype(BF16)
            p["scores"].append(lax.dot_general(qt3[i], kt, trans_b, preferred_element_type=F32).astype(BF16))

    def within_sub_chunks(p, slot):
        b3, q3 = p["b3"], p["q3"]
        shift_s[slot] = (b3 - jnp.log2(p["k3"])).reshape(LANES, dim)
        vrow_s[slot] = p["v"]
        key_row = lambda ref, s: jnp.concatenate(
            [jnp.broadcast_to(ref[slot, c * SUB + s:c * SUB + s + 1, :], (1, half, dim)) for c in range(n_sub)], axis=0)
        b_lo, b_hi, q_lo, q_hi = b3[:, :half], b3[:, half:], q3[:, :half], q3[:, half:]
        od_lo = jnp.zeros((n_sub, half, dim), F32)
        od_hi = jnp.zeros((n_sub, half, dim), F32)
        for s in range(SUB):
            bs, vs = key_row(shift_s, s), key_row(vrow_s, s)
            w_hi = jnp.sum(q_hi * jnp.exp2(b_hi - bs), axis=-1, keepdims=True)
            if s < half:
                w_lo = jnp.sum(q_lo * jnp.exp2(b_lo - bs), axis=-1, keepdims=True)
                od_lo = od_lo + jnp.where(half_row >= s, w_lo, 0.0) * vs
                od_hi = od_hi + w_hi * vs
            else:
                od_hi = od_hi + jnp.where(half_row >= s - half, w_hi, 0.0) * vs
        p["o_sub"] = jnp.concatenate([od_lo, od_hi], axis=1).reshape(LANES, dim)

    def tiles(g, st):
        group = [prepare(g * TILE_UNROLL + u) for u in range(TILE_UNROLL)]
        for p in group:
            split(p)
            first_products(p)
        for p in group:
            p["o_state"] = lax.dot_general((p["q"] * jnp.exp2(p["cum"])).astype(BF16), st.astype(BF16), trans_b,
                                           preferred_element_type=F32)
            st = st * jnp.exp2(p["c_end"]) + p["upd"]
        for u, p in enumerate(group):
            within_sub_chunks(p, u)
        for p in group:
            cross = [jnp.dot(a, p["vb"][0:(i + 1) * SUB], preferred_element_type=F32) for i, a in enumerate(p["scores"])]
            o = p["o_sub"] + p["o_state"] + jnp.concatenate([jnp.zeros((SUB, dim), F32)] + cross, axis=0)
            o_ref[0, p["rows"], :] = _gated_out(o, gw, og_ref[0, p["rows"], :])
        return st

    st = lax.fori_loop(0, t_len // (LANES * TILE_UNROLL), tiles, jnp.zeros((dim, dim), F32))
    s_ref[0, 0] = st.T


def _hgrn_tri():
    r = np.arange(LANES)
    return jnp.asarray((r[None, :] <= r[:, None]).astype(np.float32), BF16)


def _hgrn_prompt(hg, lb_logits, gw):
    b, s, _ = hg.shape
    assert s % (LANES * TILE_UNROLL) == 0
    col = lambda part: pl.BlockSpec((1, s, HGRN_DIM), lambda i, h, part=part: (i, 0, part * HGRN_HEADS + h))
    return pl.pallas_call(
        _hgrn_prompt_kernel,
        grid=(b, HGRN_HEADS),
        in_specs=[col(0), col(1), col(2), col(3),
                  pl.BlockSpec((lb_logits.shape[0], HGRN_DIM), lambda i, h: (0, h)),
                  _const_spec(gw.shape), _const_spec((LANES, LANES))],
        out_specs=[pl.BlockSpec((1, s, HGRN_DIM), lambda i, h: (i, 0, h)),
                   pl.BlockSpec((1, 1, HGRN_DIM, HGRN_DIM), lambda i, h: (i, h, 0, 0))],
        out_shape=[jax.ShapeDtypeStruct((b, s, HGRN_WIDTH), F32),
                   jax.ShapeDtypeStruct((b, HGRN_HEADS, HGRN_DIM, HGRN_DIM), F32)],
        scratch_shapes=[pltpu.VMEM((TILE_UNROLL, LANES, HGRN_DIM), F32), pltpu.VMEM((TILE_UNROLL, LANES, HGRN_DIM), F32)],
        compiler_params=_params(2),
        name="hgrn_prompt",
    )(hg, hg, hg, hg, lb_logits, gw, _hgrn_tri())


def _hgrn_sample_kernel(xt_ref, hg_ref, lblt_ref, gw_ref, st_ref, o_ref, so_ref):
    n = pl.program_id(0)
    width = HGRN_WIDTH
    xt = xt_ref[0]
    pick = lax.broadcasted_iota(I32, xt.shape, 1) == n
    cols = jnp.sum(jnp.where(pick, xt, 0.0), axis=1, keepdims=True)
    lb = _lower_bound(lblt_ref[...], 1)
    hg = hg_ref[0]
    for h in range(HGRN_HEADS):
        rows = slice(h * HGRN_DIM, (h + 1) * HGRN_DIM)
        qr = cols[rows]
        q = qr * _sigmoid(qr)
        lbh = lb[rows]
        f = lbh + (1.0 - lbh) * _sigmoid(cols[width + h * HGRN_DIM:width + (h + 1) * HGRN_DIM])
        v = hg[:, 2 * width + h * HGRN_DIM:2 * width + (h + 1) * HGRN_DIM]
        og = hg[:, 3 * width + h * HGRN_DIM:3 * width + (h + 1) * HGRN_DIM]
        s_new = f * st_ref[0, h] + (1.0 - f) * v
        so_ref[0, h] = s_new
        o = jnp.sum(q * s_new, axis=0, keepdims=True)
        o_ref[0, :, rows] = _gated_out(o, gw_ref[...], og)


def _hgrn_sample(xt, hg, lb_logits_t, gw, state):
    db = state.shape[0]
    return pl.pallas_call(
        _hgrn_sample_kernel,
        grid=(db,),
        in_specs=[_const_spec(xt.shape),
                  pl.BlockSpec((1, 1, hg.shape[2]), lambda n: (n, 0, 0)),
                  _const_spec(lb_logits_t.shape), _const_spec(gw.shape),
                  pl.BlockSpec((1,) + state.shape[1:], lambda n: (n, 0, 0, 0))],
        out_specs=[pl.BlockSpec((1, 1, HGRN_WIDTH), lambda n: (n, 0, 0)),
                   pl.BlockSpec((1,) + state.shape[1:], lambda n: (n, 0, 0, 0))],
        out_shape=[jax.ShapeDtypeStruct((db, 1, HGRN_WIDTH), F32),
                   jax.ShapeDtypeStruct(state.shape, F32)],
        compiler_params=_params(1),
        name="hgrn_sample",
    )(xt, hg, lb_logits_t, gw, state)


def _post_kernel(x_ref, oa_ref, oh_ref, gt_ref, ada_ref, wua_ref, wuh_ref, wo_ref, nf_ref, w1_ref, w2_ref, fn_ref, y_ref):
    d = x_ref.shape[-1]
    ada = ada_ref[0]
    g = gt_ref[0]
    ua = jnp.dot(oa_ref[0].astype(BF16), wua_ref[...], preferred_element_type=F32)
    uh = jnp.dot(oh_ref[0].astype(BF16), wuh_ref[...], preferred_element_type=F32)
    merged = _sigmoid(g[:, :d]) * ua + _sigmoid(g[:, d:]) * uh
    x1 = x_ref[0] + ada[:, 2 * d:3 * d] * jnp.dot(merged.astype(BF16), wo_ref[...], preferred_element_type=F32)
    h2 = _rms_mod(x1, nf_ref[...], ada[:, 3 * d:4 * d], ada[:, 4 * d:5 * d]).astype(BF16)
    a = jnp.maximum(jnp.dot(h2, w1_ref[...], preferred_element_type=F32), 0.0)
    x2 = x1 + ada[:, 5 * d:6 * d] * jnp.dot((a * a).astype(BF16), w2_ref[...], preferred_element_type=F32)
    var = jnp.mean(x2 * x2, axis=-1, keepdims=True)
    y_ref[0] = x2 * lax.rsqrt(var + EPS) * fn_ref[...]


def _post(x, oa, oh, gt, ada, wua, wuh, wo, nf, w1, w2, fn, tm):
    b, s, d = x.shape
    r = ada.shape[1]
    assert s % tm == 0 and r in (1, s) and (r == 1 or tm == s)
    row = lambda w: pl.BlockSpec((1, tm, w), lambda i, j: (i, j, 0))
    return pl.pallas_call(
        _post_kernel,
        grid=(b, s // tm),
        in_specs=[row(d), row(oa.shape[2]), row(oh.shape[2]), row(gt.shape[2]),
                  pl.BlockSpec((1, r if r == 1 else tm, N_ADA * d), lambda i, j: (i, 0, 0)),
                  _const_spec(wua.shape), _const_spec(wuh.shape), _const_spec(wo.shape), _const_spec(nf.shape),
                  _const_spec(w1.shape), _const_spec(w2.shape), _const_spec(fn.shape)],
        out_specs=row(d),
        out_shape=jax.ShapeDtypeStruct((b, s, d), F32),
        compiler_params=_params(2),
        name="post",
    )(x, oa, oh, gt, ada, wua, wuh, wo, nf, w1, w2, fn)


def _kpass_kernel(pt_ref, qt_ref, *refs):
    k_refs, sc_ref, qb_ref = refs[:PAGES_PER_STEP], refs[PAGES_PER_STEP], refs[PAGES_PER_STEP + 1]
    n = pl.program_id(0)

    @pl.when(pl.program_id(1) == 0)
    def _():
        qt = qt_ref[0]
        pick = lax.broadcasted_iota(I32, qt.shape, 1) == n
        qb_ref[...] = jnp.broadcast_to(jnp.sum(jnp.where(pick, qt, 0.0), axis=1, keepdims=True), qb_ref.shape)

    for h in range(ATTN_HEADS):
        qh = qb_ref[h * ATTN_HEAD_DIM:(h + 1) * ATTN_HEAD_DIM, :]
        for j in range(PAGES_PER_STEP):
            sc_ref[0, h:h + 1, j * PAGE_SIZE:(j + 1) * PAGE_SIZE] = jnp.sum(k_refs[j][0, h] * qh, axis=0, keepdims=True)


def _kpass(page_table, qt, cache_kt):
    db, n_pages = page_table.shape
    assert n_pages % PAGES_PER_STEP == 0
    page_spec = lambda j: pl.BlockSpec((1, ATTN_HEADS, ATTN_HEAD_DIM, PAGE_SIZE),
                                       lambda n, g, pt, j=j: (pt[n, g * PAGES_PER_STEP + j], 0, 0, 0))
    return pl.pallas_call(
        _kpass_kernel,
        grid_spec=pltpu.PrefetchScalarGridSpec(
            num_scalar_prefetch=1,
            grid=(db, n_pages // PAGES_PER_STEP),
            in_specs=[pl.BlockSpec(qt.shape, lambda n, g, pt: (0, 0, 0))] + [page_spec(j) for j in range(PAGES_PER_STEP)],
            out_specs=pl.BlockSpec((1, ATTN_HEADS, PAGES_PER_STEP * PAGE_SIZE), lambda n, g, pt: (n, 0, g)),
            scratch_shapes=[pltpu.VMEM((ATTN_WIDTH, PAGE_SIZE), F32)]),
        out_shape=jax.ShapeDtypeStruct((db, ATTN_HEADS, n_pages * PAGE_SIZE), F32),
        compiler_params=_params(2),
        name="sample_kpass",
    )(page_table, qt, *([cache_kt] * PAGES_PER_STEP))


def _select_kernel(sc_ref, qt_ref, kt_ref, ts_ref, ps_ref, idx_ref, po_ref):
    n = pl.program_id(0)
    sc = sc_ref[0]
    n_blk = sc.shape[1] // MOBA_BLOCK
    lane = lax.broadcasted_iota(I32, (ATTN_HEADS, LANES), 1).astype(F32)
    bs = jnp.full((ATTN_HEADS, LANES), -jnp.inf, F32)
    for b in range(n_blk):
        tot = jnp.sum(sc[:, b * MOBA_BLOCK:(b + 1) * MOBA_BLOCK], axis=1, keepdims=True)
        bs = jnp.where(lane == float(b), tot * (1.0 / MOBA_BLOCK), bs)
    picks = []
    for _ in range(MOBA_TOPK):
        m = jnp.max(bs, axis=1, keepdims=True)
        i = jnp.min(jnp.where(bs == m, lane, float(LANES)), axis=1, keepdims=True)
        picks.append(i)
        bs = jnp.where(lane == i, -jnp.inf, bs)
    ts = ts_ref[...].reshape(ATTN_HEADS, MOBA_BLOCK + LANES)
    logits = []
    for i in picks:
        blk = jnp.zeros((ATTN_HEADS, MOBA_BLOCK), F32)
        for b in range(n_blk):
            blk = jnp.where(i == float(b), sc[:, b * MOBA_BLOCK:(b + 1) * MOBA_BLOCK], blk)
        logits.append(blk * SCALE + jnp.where(i == float(n_blk - 1), ts[:, :MOBA_BLOCK], 0.0))
    lg = jnp.concatenate(logits, axis=1)
    prod = qt_ref[0] * kt_ref[0].reshape(ATTN_WIDTH, -1)
    own_all = jnp.sum(prod.reshape(ATTN_HEADS, ATTN_HEAD_DIM, prod.shape[1]), axis=1)
    pick = lax.broadcasted_iota(I32, own_all.shape, 1) == n
    own = jnp.sum(jnp.where(pick, own_all, 0.0), axis=1, keepdims=True) * SCALE + ts[:, MOBA_BLOCK:MOBA_BLOCK + 1]
    m = jnp.maximum(jnp.max(lg, axis=1, keepdims=True), own)
    e = jnp.exp(lg - m)
    eo = jnp.exp(own - m)
    den = jnp.sum(e, axis=1, keepdims=True) + eo
    ps_ref[0] = e / den
    po_ref[0] = jnp.broadcast_to(eo / den, (ATTN_HEADS, LANES))
    idx = jnp.zeros((ATTN_HEADS, LANES), F32)
    for t, i in enumerate(picks):
        idx = jnp.where(lane == float(t), i, idx)
    idx_ref[0] = idx.astype(I32)


def _select(scores, qt, kt_new, ts):
    db, nh, past = scores.shape
    assert past % MOBA_BLOCK == 0 and MOBA_TOPK <= past // MOBA_BLOCK <= LANES
    wsel = MOBA_TOPK * MOBA_BLOCK
    per = lambda w: pl.BlockSpec((1, nh, w), lambda n: (n, 0, 0))
    return pl.pallas_call(
        _select_kernel,
        grid=(db,),
        in_specs=[per(past), _const_spec(qt.shape), _const_spec(kt_new.shape), _const_spec(ts.shape)],
        out_specs=[per(wsel), per(LANES), per(LANES)],
        out_shape=[jax.ShapeDtypeStruct((db, nh, wsel), F32),
                   jax.ShapeDtypeStruct((db, nh, LANES), I32),
                   jax.ShapeDtypeStruct((db, nh, LANES), F32)],
        compiler_params=_params(1),
        name="sample_select",
    )(scores, qt, kt_new, ts)


N_SEL_PAGES = MOBA_TOPK * (MOBA_BLOCK // PAGE_SIZE)


def _vpass_kernel(pp_ref, ps_ref, po_ref, vn_ref, *refs):
    n_pages = ATTN_HEADS * N_SEL_PAGES
    v_refs, o_ref = refs[:n_pages], refs[n_pages]
    ones = jnp.ones((SUBLANES, PAGE_SIZE), F32)
    for h in range(ATTN_HEADS):
        ps = ps_ref[0, h]
        acc = jnp.zeros((ATTN_HEAD_DIM, PAGE_SIZE), F32)
        for j in range(N_SEL_PAGES):
            acc = acc + v_refs[h * N_SEL_PAGES + j][0, 0] * ps[:, j * PAGE_SIZE:(j + 1) * PAGE_SIZE]
        o = lax.dot_general(ones, acc, (((1,), (1,)), ((), ())), preferred_element_type=F32, precision=HIGHEST)
        o_ref[0, h] = o[0:1] + po_ref[0, h][:, 0:ATTN_HEAD_DIM] * vn_ref[0, h]


def _vpass(phys, p_sel, p_own, v_new, cache_vt):
    db, nh = p_sel.shape[:2]
    page_spec = lambda h, j: pl.BlockSpec((1, 1, ATTN_HEAD_DIM, PAGE_SIZE),
                                          lambda n, pp, h=h, j=j: (pp[(n * nh + h) * N_SEL_PAGES + j], h, 0, 0))
    per = lambda w: pl.BlockSpec((1, nh, 1, w), lambda n, pp: (n, 0, 0, 0))
    n_pages = nh * N_SEL_PAGES
    return pl.pallas_call(
        _vpass_kernel,
        grid_spec=pltpu.PrefetchScalarGridSpec(
            num_scalar_prefetch=1,
            grid=(db,),
            in_specs=[per(p_sel.shape[3]), per(LANES), per(ATTN_HEAD_DIM)]
                     + [page_spec(h, j) for h in range(nh) for j in range(N_SEL_PAGES)],
            out_specs=per(ATTN_HEAD_DIM)),
        out_shape=jax.ShapeDtypeStruct((db, nh, 1, ATTN_HEAD_DIM), F32),
        compiler_params=_params(1),
        name="sample_vpass",
    )(phys, p_sel, p_own, v_new, *([cache_vt] * n_pages))


def kernel(x_prompt, x_sample, cache_k, cache_v, state_hgrn, page_table, c_prompt, c_sample, rel_bias, hgrn_lb_logits,
           w_ada, b_ada, norm_mix_w, w_in, hgrn_gnorm_w, w_up_attn, w_up_hgrn, w_out, norm_ffn_w, w_ff1, w_ff2, final_norm_w):
    depth = w_in.shape[0]
    assert depth == 1, "one-layer trunk"
    b, s, d = x_prompt.shape
    db, ds, _ = x_sample.shape
    assert ds == 1
    past = page_table.shape[1] * PAGE_SIZE

    w = w_in[0]
    a0, a1, a2 = ATTN_WIDTH, 2 * ATTN_WIDTH, 3 * ATTN_WIDTH
    h0 = a2 + 4 * HGRN_WIDTH
    wq = w[:, :a0].astype(BF16)
    wh = w[:, a2:h0].astype(BF16)
    wg = w[:, h0:].astype(BF16)
    wt_kv = w[:, a0:a2].T.astype(BF16)
    wt_s = jnp.concatenate([w[:, a0:a2], w[:, :a0], w[:, a2:a2 + 2 * HGRN_WIDTH]], axis=1).T.astype(BF16)
    wua, wuh, wo = w_up_attn[0].astype(BF16), w_up_hgrn[0].astype(BF16), w_out[0].astype(BF16)
    w1, w2 = w_ff1[0].astype(BF16), w_ff2[0].astype(BF16)
    fn = final_norm_w.reshape(1, d)

    ada = _ada(jnp.concatenate([c_prompt, c_sample], axis=0), w_ada[0], b_ada)
    ada_p = ada[:b].reshape(b, 1, N_ADA * d)
    ada_s = ada[b:].reshape(1, db, N_ADA * d)
    t_own, t_prev, t_samp = _bias_tables(rel_bias)

    qa, hg, gt, kt, vt = _inproj(x_prompt, ada_p, norm_mix_w, wq, wh, wg, wt_kv, ROW_TILE)
    o_attn = _attn_prompt(qa, kt, vt, t_own, t_prev)
    o_hgrn, state_p = _hgrn_prompt(hg, hgrn_lb_logits, hgrn_gnorm_w)
    y_prompt = _post(x_prompt, o_attn, o_hgrn, gt, ada_p, wua, wuh, wo, norm_ffn_w, w1, w2, fn, ROW_TILE)

    xs = x_sample.reshape(1, db, d)
    qa_s, hg_s, gt_s, kt_s, vt_s, qt_s, qht_s, fht_s = _inproj(xs, ada_s, norm_mix_w, wq, wh, wg, wt_s, db)
    cache_kt = jnp.transpose(cache_k[0], (0, 2, 3, 1))
    cache_vt = jnp.transpose(cache_v[0], (0, 2, 3, 1))
    scores = _kpass(page_table, qt_s, cache_kt)
    p_sel, idx, p_own = _select(scores, qt_s, kt_s, t_samp)
    ppb = MOBA_BLOCK // PAGE_SIZE
    logical = idx[:, :, :MOBA_TOPK, None] * ppb + jnp.arange(ppb, dtype=I32)
    phys = jnp.take_along_axis(page_table, logical.reshape(db, -1), axis=1).reshape(-1)
    v_new = jnp.transpose(vt_s[0], (2, 0, 1))[:, :, None, :]
    o_attn_s = _vpass(phys, p_sel[:, :, None, :], p_own[:, :, None, :], v_new, cache_vt).reshape(1, db, ATTN_WIDTH)
    xt = jnp.concatenate([qht_s, fht_s], axis=1)
    o_hgrn_s, state_s = _hgrn_sample(xt, hg_s.reshape(db, 1, -1), hgrn_lb_logits.T, hgrn_gnorm_w, state_hgrn[0])
    y_sample = _post(xs, o_attn_s, o_hgrn_s.reshape(1, db, HGRN_WIDTH), gt_s, ada_s, wua, wuh, wo, norm_ffn_w, w1, w2, fn, db)

    to_rows = lambda t: jnp.transpose(t, (0, 3, 1, 2))[None]
    return (y_prompt, y_sample.reshape(db, 1, d), to_rows(kt), to_rows(vt), state_p[None],
            to_rows(kt_s).reshape(1, db, 1, ATTN_HEADS, ATTN_HEAD_DIM), to_rows(vt_s).reshape(1, db, 1, ATTN_HEADS, ATTN_HEAD_DIM),
            state_s[None])
```

```python
import functools
import math

import numpy as np
import jax
import jax.numpy as jnp
from jax import lax
from jax.experimental import pallas as pl
from jax.experimental.pallas import tpu as pltpu

F32, BF16, I32 = jnp.float32, jnp.bfloat16, jnp.int32
HIGHEST = lax.Precision.HIGHEST

ATTN_HEADS = 8
ATTN_HEAD_DIM = 64
ATTN_WIDTH = ATTN_HEADS * ATTN_HEAD_DIM
MOBA_BLOCK = 256
MOBA_TOPK = 3
PAGE_SIZE = 128
N_BUCKETS = 32
MAX_DISTANCE = 128
HGRN_HEADS = 4
HGRN_DIM = 128
HGRN_WIDTH = HGRN_HEADS * HGRN_DIM
N_ADA = 6
EPS = 1e-6
SCALE = ATTN_HEAD_DIM ** -0.5
LOG2E = math.log2(math.e)

LANES = 128
SUBLANES = 8
VMEM_BYTES_V7X = 64 * 1024 * 1024
VMEM_LIMIT = VMEM_BYTES_V7X * 7 // 8

NEG = -1e30
SUB = 16
TILE_UNROLL = 4
ROW_TILE = 256
PAGES_PER_STEP = 16
PAGE_SLOTS = 3
SAMPLES_PER_STEP = 8
PAIR = 2 * ATTN_HEAD_DIM
PAIR_SLOTS = 8
V_PAD_ROWS = 16


def _params(n_grid):
    return pltpu.CompilerParams(dimension_semantics=("arbitrary",) * n_grid, vmem_limit_bytes=VMEM_LIMIT)


def _const_spec(shape):
    nd = len(shape)
    return pl.BlockSpec(shape, lambda *_: (0,) * nd, pipeline_mode=pl.Buffered(1))


def _sigmoid(x):
    return jax.nn.sigmoid(x)


def _rms_mod(x, w, shift, scale):
    var = jnp.mean(x * x, axis=-1, keepdims=True)
    return (x * lax.rsqrt(var + EPS) * w) * (1.0 + scale) + shift


def _ada_kernel(c_ref, w_ref, b_ref, o_ref):
    c = c_ref[...]
    o_ref[...] = jnp.dot(c * _sigmoid(c), w_ref[...], preferred_element_type=F32, precision=HIGHEST) + b_ref[...]


def _ada(c_all, w_ada, b_ada):
    n, d = c_all.shape
    width = w_ada.shape[1]
    tn = 1536
    assert width % tn == 0
    return pl.pallas_call(
        _ada_kernel,
        grid=(width // tn,),
        in_specs=[pl.BlockSpec((n, d), lambda j: (0, 0)),
                  pl.BlockSpec((d, tn), lambda j: (0, j)),
                  pl.BlockSpec((1, tn), lambda j: (0, j))],
        out_specs=pl.BlockSpec((n, tn), lambda j: (0, j)),
        out_shape=jax.ShapeDtypeStruct((n, width), F32),
        compiler_params=_params(1),
        name="ada",
    )(c_all, w_ada, b_ada)


def _bucket_np(dist):
    n = np.maximum(dist, 0)
    max_exact = N_BUCKETS // 2
    nf = np.maximum(n, max_exact).astype(np.float32)
    large = max_exact + (np.log(nf / np.float32(max_exact)) / np.float32(math.log(MAX_DISTANCE / max_exact))
                         * np.float32(N_BUCKETS - max_exact)).astype(np.int32)
    large = np.minimum(large, N_BUCKETS - 1)
    return np.where(n < max_exact, n, large).astype(np.int32)


def _bucket_tables():
    i = np.arange(MOBA_BLOCK)[:, None]
    j = np.arange(MOBA_BLOCK)[None, :]
    own = np.where(j <= i, _bucket_np(i - j), -1).astype(np.int32)
    prev = _bucket_np(MOBA_BLOCK + i - j)
    assert int(_bucket_np(np.array([MOBA_BLOCK + 1]))[0]) == N_BUCKETS - 1
    samp = np.full((1, MOBA_BLOCK + LANES), N_BUCKETS - 1, np.int32)
    samp[0, :MOBA_BLOCK] = _bucket_np(MOBA_BLOCK - np.arange(MOBA_BLOCK))
    samp[0, MOBA_BLOCK] = 0
    return own, prev, samp


def _bias_kernel(rb_ref, bo_ref, bp_ref, bs_ref, to_ref, tp_ref, ts_ref):
    h = pl.program_id(0)
    far = rb_ref[N_BUCKETS - 1, h]

    def table(bucket):
        acc = jnp.where(bucket < 0, NEG, 0.0).astype(F32)
        for k in range(N_BUCKETS - 1):
            acc = jnp.where(bucket == k, rb_ref[k, h] - far, acc)
        return acc

    to_ref[0] = table(bo_ref[...]) * LOG2E
    tp_ref[0] = table(bp_ref[...]) * LOG2E
    ts_ref[0] = table(bs_ref[...])


def _bias_tables(rel_bias):
    own, prev, samp = _bucket_tables()
    nh = rel_bias.shape[1]
    b = MOBA_BLOCK
    return pl.pallas_call(
        _bias_kernel,
        grid=(nh,),
        in_specs=[pl.BlockSpec(memory_space=pltpu.SMEM),
                  pl.BlockSpec((b, b), lambda h: (0, 0)),
                  pl.BlockSpec((b, b), lambda h: (0, 0)),
                  pl.BlockSpec((1, b + LANES), lambda h: (0, 0))],
        out_specs=[pl.BlockSpec((1, b, b), lambda h: (h, 0, 0)),
                   pl.BlockSpec((1, b, b), lambda h: (h, 0, 0)),
                   pl.BlockSpec((1, 1, b + LANES), lambda h: (h, 0, 0))],
        out_shape=[jax.ShapeDtypeStruct((nh, b, b), F32),
                   jax.ShapeDtypeStruct((nh, b, b), F32),
                   jax.ShapeDtypeStruct((nh, 1, b + LANES), F32)],
        compiler_params=_params(1),
        name="bias_tables",
    )(rel_bias, jnp.asarray(own.T), jnp.asarray(prev.T), jnp.asarray(samp))


def _inproj_kernel(n_t, x_ref, ada_ref, nw_ref, wq_ref, wh_ref, wg_ref, wt_ref, qa_ref, hg_ref, gt_ref, *t_refs):
    d = x_ref.shape[-1]
    ada = ada_ref[0]
    h = _rms_mod(x_ref[0], nw_ref[...], ada[:, 0:d], ada[:, d:2 * d]).astype(BF16)
    qa_ref[0] = jnp.dot(h, wq_ref[...], preferred_element_type=F32)
    hg_ref[0] = jnp.dot(h, wh_ref[...], preferred_element_type=F32)
    gt_ref[0] = jnp.dot(h, wg_ref[...], preferred_element_type=F32)
    t = lax.dot_general(wt_ref[...], h, (((1,), (1,)), ((), ())), preferred_element_type=F32)
    for i in range(n_t):
        ref = t_refs[i]
        ref[0] = t[i * ATTN_WIDTH:(i + 1) * ATTN_WIDTH].reshape(ref.shape[1:])


def _inproj(x, ada, nw, wq, wh, wg, wt, tm):
    b, s, d = x.shape
    r = ada.shape[1]
    n_t = wt.shape[0] // ATTN_WIDTH
    assert s % tm == 0 and r in (1, s) and (r == 1 or tm == s)
    t_shapes, t_specs = [], []
    for i in range(n_t):
        if i < 2:
            t_shapes.append(jax.ShapeDtypeStruct((b, ATTN_HEADS, ATTN_HEAD_DIM, s), F32))
            t_specs.append(pl.BlockSpec((1, ATTN_HEADS, ATTN_HEAD_DIM, tm), lambda i_, j_: (i_, 0, 0, j_)))
        else:
            t_shapes.append(jax.ShapeDtypeStruct((b, ATTN_WIDTH, s), F32))
            t_specs.append(pl.BlockSpec((1, ATTN_WIDTH, tm), lambda i_, j_: (i_, 0, j_)))
    row = lambda w: pl.BlockSpec((1, tm, w), lambda i_, j_: (i_, j_, 0))
    return pl.pallas_call(
        functools.partial(_inproj_kernel, n_t),
        grid=(b, s // tm),
        in_specs=[row(d),
                  pl.BlockSpec((1, r if r == 1 else tm, N_ADA * d), lambda i_, j_: (i_, 0, 0)),
                  _const_spec(nw.shape), _const_spec(wq.shape), _const_spec(wh.shape), _const_spec(wg.shape),
                  _const_spec(wt.shape)],
        out_specs=[row(wq.shape[1]), row(wh.shape[1]), row(wg.shape[1])] + t_specs,
        out_shape=[jax.ShapeDtypeStruct((b, s, wq.shape[1]), F32),
                   jax.ShapeDtypeStruct((b, s, wh.shape[1]), F32),
                   jax.ShapeDtypeStruct((b, s, wg.shape[1]), F32)] + t_shapes,
        compiler_params=_params(2),
        name="inproj",
    )(x, ada, nw, wq, wh, wg, wt)


def _attn_consts(s):
    nb = s // MOBA_BLOCK
    key_blk = np.arange(s) // MOBA_BLOCK
    gm = np.zeros((LANES, LANES), np.float32)
    for b in range(nb):
        for b2 in range(nb):
            gm[b, b * PAIR_SLOTS + b2] = 1.0
    er = np.zeros((s, LANES), np.float32)
    er[np.arange(s), key_blk] = 1.0
    return jnp.asarray(gm, BF16), jnp.asarray(er, BF16)


def _attn_prompt_kernel(spp, pt_ref, q_ref, kt_ref, vt_ref, to_ref, tp_ref, gm_ref, er_ref, qts_ref, ktn_ref, ts_ref, cache_ref,
                        o_ref, ps_ref, idx_ref, po_ref, kaug_ref, vaug_ref, qb_ref, sc_ref, pages_ref, sem):
    s_len = q_ref.shape[1]
    nb = s_len // MOBA_BLOCK
    blk = MOBA_BLOCK

    step = pl.program_id(0) * pl.num_programs(1) + pl.program_id(1)
    n_batches = pt_ref.shape[1] // PAGES_PER_STEP
    work = [(u, g) for u in range(spp) for g in range(n_batches)]

    def batch_copies(w):
        u, g = work[w]
        n = step * spp + u
        slot = w % PAGE_SLOTS
        return [pltpu.make_async_copy(cache_ref.at[pt_ref[n, g * PAGES_PER_STEP + j]], pages_ref.at[slot, j], sem.at[slot])
                for j in range(PAGES_PER_STEP)]

    def start_batch(w):
        if w < len(work):
            for c in batch_copies(w):
                c.start()

    def finish_batch(w):
        u, g = work[w]
        n = step * spp + u
        for c in batch_copies(w):
            c.wait()
        start_batch(w + PAGE_SLOTS - 1)
        if g == 0:
            _sample_query(qts_ref, n, qb_ref)
        _page_scores(pages_ref.at[w % PAGE_SLOTS], qb_ref, sc_ref, g * PAGES_PER_STEP)
        if g == n_batches - 1:
            _select_blocks(sc_ref[...], n, qts_ref, ktn_ref, ts_ref, ps_ref, idx_ref, po_ref, u)

    order = list(range(nb - 1, -1, -1))
    due = [min(nb - 1, (w * max(1, nb // 2)) // len(work)) for w in range(len(work))]
    for w in range(PAGE_SLOTS - 1):
        start_batch(w)
    k_rows = kt_ref[0].reshape(PAIR, s_len).T
    kaug_ref[:, 0:PAIR] = k_rows.astype(BF16)
    kaug_ref[:, PAIR:2 * PAIR] = er_ref[...]
    vaug_ref[0:PAIR, :] = vt_ref[0].reshape(PAIR, s_len).astype(BF16)
    vaug_ref[PAIR:, :] = jnp.where(lax.broadcasted_iota(I32, (V_PAD_ROWS, s_len), 0) == 0, 1.0, 0.0).astype(BF16)
    q_t = q_ref[0].T * (SCALE * LOG2E)
    dim_row = lax.broadcasted_iota(I32, (PAIR, blk), 0)
    slot = lax.broadcasted_iota(I32, (LANES, 2 * blk), 0)
    t_own = jnp.concatenate([to_ref[0], to_ref[1]], axis=1)
    t_prev = jnp.concatenate([tp_ref[0], tp_ref[1]], axis=1)

    q_hi, q_lo = [], []
    for j in range(nb):
        q2 = q_t[:, j * blk:(j + 1) * blk]
        qs = jnp.concatenate([jnp.where(dim_row < ATTN_HEAD_DIM, q2, 0.0), jnp.where(dim_row >= ATTN_HEAD_DIM, q2, 0.0)], axis=1)
        q_hi.append(qs.astype(BF16))
        q_lo.append((qs - q_hi[j].astype(F32)).astype(BF16) if j > MOBA_TOPK else None)

    mask_rows = [None] * nb
    if nb > MOBA_TOPK + 1:
        slot_k = lax.broadcasted_iota(I32, (LANES, PAIR), 0)
        dmean = jnp.zeros((LANES, PAIR), F32)
        for b in range(nb - 1):
            mb = jnp.sum(k_rows[b * blk:(b + 1) * blk], axis=0, keepdims=True) * (1.0 / blk)
            dmean = dmean + jnp.where(slot_k % PAIR_SLOTS == b, mb, 0.0) - jnp.where(slot_k // PAIR_SLOTS == b, mb, 0.0)
        d_hi = dmean.astype(BF16)
        d_lo = (dmean - d_hi.astype(F32)).astype(BF16)
        d_stack = jnp.concatenate([d_hi, d_hi, d_lo], axis=1)
        blk_b = slot // PAIR_SLOTS
        blk_b2 = slot % PAIR_SLOTS
        diffs = {j: jnp.dot(d_stack, jnp.concatenate([q_hi[j], q_lo[j], q_hi[j]], axis=0), preferred_element_type=F32)
                 for j in range(MOBA_TOPK + 1, nb)}
        ranks = {}
        for j, diff in diffs.items():
            beats = ((diff > 0.0) | ((diff == 0.0) & (blk_b2 < blk_b))) & (blk_b2 < j) & (blk_b < j)
            ranks[j] = jnp.dot(gm_ref[...], jnp.where(beats, 1.0, 0.0).astype(BF16), preferred_element_type=F32)
        for j, rank in ranks.items():
            keep = (slot == j) | ((slot < j) & (rank < MOBA_TOPK)) | (slot >= PAIR_SLOTS)
            mask_rows[j] = jnp.where(keep, 0.0, NEG).astype(BF16)

    def scores(j):
        n_keys = (j + 1) * blk
        if mask_rows[j] is None:
            return jnp.dot(kaug_ref[0:n_keys, 0:PAIR], q_hi[j], preferred_element_type=F32)
        return jnp.dot(kaug_ref[0:n_keys, :], jnp.concatenate([q_hi[j], mask_rows[j]], axis=0), preferred_element_type=F32)

    sc_next = scores(order[0])
    for pos, j in enumerate(order):
        n_keys = (j + 1) * blk
        sc = sc_next
        if pos + 1 < nb:
            sc_next = scores(order[pos + 1])
        pieces = []
        if j >= 2:
            pieces.append(sc[:n_keys - 2 * blk])
        if j >= 1:
            pieces.append(sc[n_keys - 2 * blk:n_keys - blk] + t_prev)
        pieces.append(sc[n_keys - blk:] + t_own)
        sc = jnp.concatenate(pieces, axis=0) if len(pieces) > 1 else pieces[0]
        m = jnp.max(sc, axis=0, keepdims=True)
        p = jnp.exp2(sc - m).astype(BF16)
        o2 = jnp.dot(vaug_ref[:, 0:n_keys], p, preferred_element_type=F32)
        o2 = o2[0:PAIR] * (1.0 / o2[PAIR:PAIR + 1])
        heads = jnp.concatenate([o2[:ATTN_HEAD_DIM, :blk], o2[ATTN_HEAD_DIM:, blk:]], axis=0)
        o_ref[0, j * blk:(j + 1) * blk, :] = heads.T
        for w in range(len(work)):
            if due[w] == pos:
                finish_batch(w)


def _attention(qa, kt, vt, t_own, t_prev, page_table, qt_s, kt_new, t_samp, cache_kt):
    b, s, _ = qa.shape
    db, n_pages = page_table.shape
    n_pairs = ATTN_HEADS // 2
    past = n_pages * PAGE_SIZE
    assert s % MOBA_BLOCK == 0 and s // MOBA_BLOCK <= PAIR_SLOTS
    assert db % (b * n_pairs) == 0 and n_pages % PAGES_PER_STEP == 0
    assert past % MOBA_BLOCK == 0 and MOBA_TOPK <= past // MOBA_BLOCK <= LANES
    spp = db // (b * n_pairs)
    gm, er = _attn_consts(s)
    wsel = MOBA_TOPK * MOBA_BLOCK
    const = lambda shape: pl.BlockSpec(shape, lambda i, p, pt: (0,) * len(shape), pipeline_mode=pl.Buffered(1))
    per_sample = lambda w: pl.BlockSpec((spp, ATTN_HEADS, w), lambda i, p, pt: (i * n_pairs + p, 0, 0))
    return pl.pallas_call(
        functools.partial(_attn_prompt_kernel, spp),
        grid_spec=pltpu.PrefetchScalarGridSpec(
            num_scalar_prefetch=1,
            grid=(b, n_pairs),
            in_specs=[pl.BlockSpec((1, s, PAIR), lambda i, p, pt: (i, 0, p)),
                      pl.BlockSpec((1, 2, ATTN_HEAD_DIM, s), lambda i, p, pt: (i, p, 0, 0)),
                      pl.BlockSpec((1, 2, ATTN_HEAD_DIM, s), lambda i, p, pt: (i, p, 0, 0)),
                      pl.BlockSpec((2, MOBA_BLOCK, MOBA_BLOCK), lambda i, p, pt: (p, 0, 0)),
                      pl.BlockSpec((2, MOBA_BLOCK, MOBA_BLOCK), lambda i, p, pt: (p, 0, 0)),
                      const(gm.shape), const(er.shape), const(qt_s.shape), const(kt_new.shape), const(t_samp.shape),
                      pl.BlockSpec(memory_space=pl.ANY)],
            out_specs=[pl.BlockSpec((1, s, PAIR), lambda i, p, pt: (i, 0, p)),
                       per_sample(wsel), per_sample(LANES), per_sample(LANES)],
            scratch_shapes=[pltpu.VMEM((s, 2 * PAIR), BF16), pltpu.VMEM((PAIR + V_PAD_ROWS, s), BF16),
                            pltpu.VMEM((ATTN_WIDTH, PAGE_SIZE), F32), pltpu.VMEM((ATTN_HEADS, past), F32),
                            pltpu.VMEM((PAGE_SLOTS, PAGES_PER_STEP, ATTN_WIDTH, PAGE_SIZE), F32),
                            pltpu.SemaphoreType.DMA((PAGE_SLOTS,))]),
        out_shape=[jax.ShapeDtypeStruct((b, s, ATTN_WIDTH), F32),
                   jax.ShapeDtypeStruct((db, ATTN_HEADS, wsel), F32),
                   jax.ShapeDtypeStruct((db, ATTN_HEADS, LANES), I32),
                   jax.ShapeDtypeStruct((db, ATTN_HEADS, LANES), F32)],
        compiler_params=_params(2),
        name="attention",
    )(page_table, qa, kt, vt, t_own, t_prev, gm, er, qt_s, kt_new, t_samp, cache_kt)


def _lower_bound(logits, axis):
    m = jnp.max(logits, axis=axis, keepdims=True)
    e = jnp.exp(logits - m)
    first = e[0:1] if axis == 0 else e[:, 0:1]
    return first / jnp.sum(e, axis=axis, keepdims=True)


def _gated_out(o, gw, og):
    var = jnp.mean(o * o, axis=-1, keepdims=True)
    return (o * lax.rsqrt(var + EPS) * gw) * (og * _sigmoid(og))


def _cumsum_rows(tri, x):
    hi = x.astype(BF16)
    rest = x - hi.astype(F32)
    mid = rest.astype(BF16)
    lo = (rest - mid.astype(F32)).astype(BF16)
    n = x.shape[1]
    y = jnp.dot(tri, jnp.concatenate([hi, mid, lo], axis=1), preferred_element_type=F32)
    return y[:, :n] + y[:, n:2 * n] + y[:, 2 * n:]


def _hgrn_prompt_kernel(q_ref, f_ref, v_ref, og_ref, lbl_ref, gw_ref, tri_ref, o_ref, s_ref, shift_s, vrow_s):
    t_len = q_ref.shape[1]
    n_sub = LANES // SUB
    half = SUB // 2
    dim = HGRN_DIM
    lb = _lower_bound(lbl_ref[...], 0)
    gw = gw_ref[...]
    tri = tri_ref[...]
    half_row = lax.broadcasted_iota(I32, (n_sub, half, dim), 1)
    trans_b = (((1,), (1,)), ((), ()))

    def prepare(t):
        rows = pl.ds(pl.multiple_of(t * LANES, LANES), LANES)
        qr = q_ref[0, rows, :]
        q = qr * _sigmoid(qr)
        f = lb + (1.0 - lb) * _sigmoid(f_ref[0, rows, :])
        v = v_ref[0, rows, :]
        cum = _cumsum_rows(tri, jnp.log2(f))
        return dict(rows=rows, q=q, kk=1.0 - f, v=v, vb=v.astype(BF16), cum=cum)

    def split(p):
        c3 = p["cum"].reshape(n_sub, SUB, dim)
        start = jnp.concatenate([jnp.zeros((1, 1, dim), F32), c3[:n_sub - 1, SUB - 1:SUB, :]], axis=0)
        p.update(c3=c3, start=start, b3=c3 - start, c_end=c3[n_sub - 1, SUB - 1:SUB, :],
                 q3=p["q"].reshape(n_sub, SUB, dim), k3=p["kk"].reshape(n_sub, SUB, dim))

    def first_products(p):
        p["upd"] = lax.dot_general(p["vb"], (p["kk"] * jnp.exp2(p["c_end"] - p["cum"])).astype(BF16),
                                   (((0,), (0,)), ((), ())), preferred_element_type=F32)
        qt3 = (p["q3"] * jnp.exp2(p["b3"])).astype(BF16)
        p["scores"] = []
        for i in range(1, n_sub):
            kt = (p["k3"][:i] * jnp.exp2(p["start"][i:i + 1] - p["c3"][:i])).reshape(i * SUB, dim).astype(BF16)
            p["scores"].append(lax.dot_general(qt3[i], kt, trans_b, preferred_element_type=F32).astype(BF16))

    def within_sub_chunks(p, slot):
        b3, q3 = p["b3"], p["q3"]
        shift_s[slot] = (b3 - jnp.log2(p["k3"])).reshape(LANES, dim)
        vrow_s[slot] = p["v"]
        key_row = lambda ref, s: jnp.concatenate(
            [jnp.broadcast_to(ref[slot, c * SUB + s:c * SUB + s + 1, :], (1, half, dim)) for c in range(n_sub)], axis=0)
        b_lo, b_hi, q_lo, q_hi = b3[:, :half], b3[:, half:], q3[:, :half], q3[:, half:]
        od_lo = jnp.zeros((n_sub, half, dim), F32)
        od_hi = jnp.zeros((n_sub, half, dim), F32)
        for s in range(SUB):
            bs, vs = key_row(shift_s, s), key_row(vrow_s, s)
            w_hi = jnp.sum(q_hi * jnp.exp2(b_hi - bs), axis=-1, keepdims=True)
            if s < half:
                w_lo = jnp.sum(q_lo * jnp.exp2(b_lo - bs), axis=-1, keepdims=True)
                od_lo = od_lo + jnp.where(half_row >= s, w_lo, 0.0) * vs
                od_hi = od_hi + w_hi * vs
            else:
                od_hi = od_hi + jnp.where(half_row >= s - half, w_hi, 0.0) * vs
        p["o_sub"] = jnp.concatenate([od_lo, od_hi], axis=1).reshape(LANES, dim)

    def tiles(g, st):
        group = [prepare(g * TILE_UNROLL + u) for u in range(TILE_UNROLL)]
        for p in group:
            split(p)
            first_products(p)
        for p in group:
            p["o_state"] = lax.dot_general((p["q"] * jnp.exp2(p["cum"])).astype(BF16), st.astype(BF16), trans_b,
                                           preferred_element_type=F32)
            st = st * jnp.exp2(p["c_end"]) + p["upd"]
        for u, p in enumerate(group):
            within_sub_chunks(p, u)
        for p in group:
            cross = [jnp.dot(a, p["vb"][0:(i + 1) * SUB], preferred_element_type=F32) for i, a in enumerate(p["scores"])]
            o = p["o_sub"] + p["o_state"] + jnp.concatenate([jnp.zeros((SUB, dim), F32)] + cross, axis=0)
            o_ref[0, p["rows"], :] = _gated_out(o, gw, og_ref[0, p["rows"], :])
        return st

    st = lax.fori_loop(0, t_len // (LANES * TILE_UNROLL), tiles, jnp.zeros((dim, dim), F32))
    s_ref[0, 0] = st.T


def _hgrn_tri():
    r = np.arange(LANES)
    return jnp.asarray((r[None, :] <= r[:, None]).astype(np.float32), BF16)


def _hgrn_prompt(hg, lb_logits, gw):
    b, s, _ = hg.shape
    assert s % (LANES * TILE_UNROLL) == 0
    col = lambda part: pl.BlockSpec((1, s, HGRN_DIM), lambda i, h, part=part: (i, 0, part * HGRN_HEADS + h))
    return pl.pallas_call(
        _hgrn_prompt_kernel,
        grid=(b, HGRN_HEADS),
        in_specs=[col(0), col(1), col(2), col(3),
                  pl.BlockSpec((lb_logits.shape[0], HGRN_DIM), lambda i, h: (0, h)),
                  _const_spec(gw.shape), _const_spec((LANES, LANES))],
        out_specs=[pl.BlockSpec((1, s, HGRN_DIM), lambda i, h: (i, 0, h)),
                   pl.BlockSpec((1, 1, HGRN_DIM, HGRN_DIM), lambda i, h: (i, h, 0, 0))],
        out_shape=[jax.ShapeDtypeStruct((b, s, HGRN_WIDTH), F32),
                   jax.ShapeDtypeStruct((b, HGRN_HEADS, HGRN_DIM, HGRN_DIM), F32)],
        scratch_shapes=[pltpu.VMEM((TILE_UNROLL, LANES, HGRN_DIM), F32), pltpu.VMEM((TILE_UNROLL, LANES, HGRN_DIM), F32)],
        compiler_params=_params(2),
        name="hgrn_prompt",
    )(hg, hg, hg, hg, lb_logits, gw, _hgrn_tri())


def _hgrn_sample_kernel(xt_ref, hg_ref, lblt_ref, gw_ref, st_ref, o_ref, so_ref):
    width = HGRN_WIDTH
    xt = xt_ref[0]
    lane = lax.broadcasted_iota(I32, xt.shape, 1)
    lb = _lower_bound(lblt_ref[...], 1)
    for u in range(st_ref.shape[0]):
        n = pl.program_id(0) * st_ref.shape[0] + u
        cols = jnp.sum(jnp.where(lane == n, xt, 0.0), axis=1, keepdims=True)
        hg = hg_ref[u]
        for h in range(HGRN_HEADS):
            rows = slice(h * HGRN_DIM, (h + 1) * HGRN_DIM)
            qr = cols[rows]
            q = qr * _sigmoid(qr)
            lbh = lb[rows]
            f = lbh + (1.0 - lbh) * _sigmoid(cols[width + h * HGRN_DIM:width + (h + 1) * HGRN_DIM])
            v = hg[:, 2 * width + h * HGRN_DIM:2 * width + (h + 1) * HGRN_DIM]
            og = hg[:, 3 * width + h * HGRN_DIM:3 * width + (h + 1) * HGRN_DIM]
            s_new = f * st_ref[u, h] + (1.0 - f) * v
            so_ref[u, h] = s_new
            o = jnp.sum(q * s_new, axis=0, keepdims=True)
            o_ref[u, :, rows] = _gated_out(o, gw_ref[...], og)


def _hgrn_sample(xt, hg, lb_logits_t, gw, state):
    db = state.shape[0]
    ns = SAMPLES_PER_STEP
    assert db % ns == 0
    return pl.pallas_call(
        _hgrn_sample_kernel,
        grid=(db // ns,),
        in_specs=[_const_spec(xt.shape),
                  pl.BlockSpec((ns, 1, hg.shape[2]), lambda n: (n, 0, 0)),
                  _const_spec(lb_logits_t.shape), _const_spec(gw.shape),
                  pl.BlockSpec((ns,) + state.shape[1:], lambda n: (n, 0, 0, 0))],
        out_specs=[pl.BlockSpec((ns, 1, HGRN_WIDTH), lambda n: (n, 0, 0)),
                   pl.BlockSpec((ns,) + state.shape[1:], lambda n: (n, 0, 0, 0))],
        out_shape=[jax.ShapeDtypeStruct((db, 1, HGRN_WIDTH), F32),
                   jax.ShapeDtypeStruct(state.shape, F32)],
        compiler_params=_params(1),
        name="hgrn_sample",
    )(xt, hg, lb_logits_t, gw, state)


def _post_kernel(x_ref, oa_ref, oh_ref, gt_ref, ada_ref, wua_ref, wuh_ref, wo_ref, nf_ref, w1_ref, w2_ref, fn_ref, y_ref):
    d = x_ref.shape[-1]
    ada = ada_ref[0]
    g = gt_ref[0]
    ua = jnp.dot(oa_ref[0].astype(BF16), wua_ref[...], preferred_element_type=F32)
    uh = jnp.dot(oh_ref[0].astype(BF16), wuh_ref[...], preferred_element_type=F32)
    merged = _sigmoid(g[:, :d]) * ua + _sigmoid(g[:, d:]) * uh
    x1 = x_ref[0] + ada[:, 2 * d:3 * d] * jnp.dot(merged.astype(BF16), wo_ref[...], preferred_element_type=F32)
    h2 = _rms_mod(x1, nf_ref[...], ada[:, 3 * d:4 * d], ada[:, 4 * d:5 * d]).astype(BF16)
    a = jnp.maximum(jnp.dot(h2, w1_ref[...], preferred_element_type=F32), 0.0)
    x2 = x1 + ada[:, 5 * d:6 * d] * jnp.dot((a * a).astype(BF16), w2_ref[...], preferred_element_type=F32)
    var = jnp.mean(x2 * x2, axis=-1, keepdims=True)
    y_ref[0] = x2 * lax.rsqrt(var + EPS) * fn_ref[...]


def _post(x, oa, oh, gt, ada, wua, wuh, wo, nf, w1, w2, fn, tm):
    b, s, d = x.shape
    r = ada.shape[1]
    assert s % tm == 0 and r in (1, s) and (r == 1 or tm == s)
    row = lambda w: pl.BlockSpec((1, tm, w), lambda i, j: (i, j, 0))
    return pl.pallas_call(
        _post_kernel,
        grid=(b, s // tm),
        in_specs=[row(d), row(oa.shape[2]), row(oh.shape[2]), row(gt.shape[2]),
                  pl.BlockSpec((1, r if r == 1 else tm, N_ADA * d), lambda i, j: (i, 0, 0)),
                  _const_spec(wua.shape), _const_spec(wuh.shape), _const_spec(wo.shape), _const_spec(nf.shape),
                  _const_spec(w1.shape), _const_spec(w2.shape), _const_spec(fn.shape)],
        out_specs=row(d),
        out_shape=jax.ShapeDtypeStruct((b, s, d), F32),
        compiler_params=_params(2),
        name="post",
    )(x, oa, oh, gt, ada, wua, wuh, wo, nf, w1, w2, fn)


def _sample_query(qt_ref, n, qb_ref):
    qt = qt_ref[0]
    pick = lax.broadcasted_iota(I32, qt.shape, 1) == n
    qb_ref[...] = jnp.broadcast_to(jnp.sum(jnp.where(pick, qt, 0.0), axis=1, keepdims=True), qb_ref.shape)


def _page_scores(pages_ref, qb_ref, sc_ref, first_page):
    for h in range(ATTN_HEADS):
        rows = slice(h * ATTN_HEAD_DIM, (h + 1) * ATTN_HEAD_DIM)
        qh = qb_ref[rows, :]
        for j in range(PAGES_PER_STEP):
            lanes = slice((first_page + j) * PAGE_SIZE, (first_page + j + 1) * PAGE_SIZE)
            sc_ref[h:h + 1, lanes] = jnp.sum(pages_ref[j, rows, :] * qh, axis=0, keepdims=True)


def _select_blocks(sc, n, qt_ref, kt_ref, ts_ref, ps_ref, idx_ref, po_ref, u):
    n_blk = sc.shape[1] // MOBA_BLOCK
    lane = lax.broadcasted_iota(I32, (ATTN_HEADS, LANES), 1).astype(F32)
    bs = jnp.full((ATTN_HEADS, LANES), -jnp.inf, F32)
    for b in range(n_blk):
        tot = jnp.sum(sc[:, b * MOBA_BLOCK:(b + 1) * MOBA_BLOCK], axis=1, keepdims=True)
        bs = jnp.where(lane == float(b), tot * (1.0 / MOBA_BLOCK), bs)
    picks = []
    for _ in range(MOBA_TOPK):
        m = jnp.max(bs, axis=1, keepdims=True)
        i = jnp.min(jnp.where(bs == m, lane, float(LANES)), axis=1, keepdims=True)
        picks.append(i)
        bs = jnp.where(lane == i, -jnp.inf, bs)
    ts = ts_ref[...].reshape(ATTN_HEADS, MOBA_BLOCK + LANES)
    logits = []
    for i in picks:
        blk = jnp.zeros((ATTN_HEADS, MOBA_BLOCK), F32)
        for b in range(n_blk):
            blk = jnp.where(i == float(b), sc[:, b * MOBA_BLOCK:(b + 1) * MOBA_BLOCK], blk)
        logits.append(blk * SCALE + jnp.where(i == float(n_blk - 1), ts[:, :MOBA_BLOCK], 0.0))
    lg = jnp.concatenate(logits, axis=1)
    prod = qt_ref[0] * kt_ref[0].reshape(ATTN_WIDTH, -1)
    own_all = jnp.sum(prod.reshape(ATTN_HEADS, ATTN_HEAD_DIM, prod.shape[1]), axis=1)
    pick = lax.broadcasted_iota(I32, own_all.shape, 1) == n
    own = jnp.sum(jnp.where(pick, own_all, 0.0), axis=1, keepdims=True) * SCALE + ts[:, MOBA_BLOCK:MOBA_BLOCK + 1]
    m = jnp.maximum(jnp.max(lg, axis=1, keepdims=True), own)
    e = jnp.exp(lg - m)
    eo = jnp.exp(own - m)
    den = jnp.sum(e, axis=1, keepdims=True) + eo
    ps_ref[u] = e / den
    po_ref[u] = jnp.broadcast_to(eo / den, (ATTN_HEADS, LANES))
    idx = jnp.zeros((ATTN_HEADS, LANES), F32)
    for t, i in enumerate(picks):
        idx = jnp.where(lane == float(t), i, idx)
    idx_ref[u] = idx.astype(I32)


N_SEL_PAGES = MOBA_TOPK * (MOBA_BLOCK // PAGE_SIZE)


def _vpass_kernel(pp_ref, ps_ref, po_ref, vn_ref, *refs):
    n_pages = ATTN_HEADS * N_SEL_PAGES
    v_refs, o_ref = refs[:n_pages], refs[n_pages]
    ones = jnp.ones((SUBLANES, PAGE_SIZE), F32)
    for h in range(ATTN_HEADS):
        ps = ps_ref[0, h]
        acc = jnp.zeros((ATTN_HEAD_DIM, PAGE_SIZE), F32)
        for j in range(N_SEL_PAGES):
            acc = acc + v_refs[h * N_SEL_PAGES + j][0, 0] * ps[:, j * PAGE_SIZE:(j + 1) * PAGE_SIZE]
        o = lax.dot_general(ones, acc, (((1,), (1,)), ((), ())), preferred_element_type=F32, precision=HIGHEST)
        o_ref[0, h] = o[0:1] + po_ref[0, h][:, 0:ATTN_HEAD_DIM] * vn_ref[0, h]


def _vpass(phys, p_sel, p_own, v_new, cache_vt):
    db, nh = p_sel.shape[:2]
    page_spec = lambda h, j: pl.BlockSpec((1, 1, ATTN_HEAD_DIM, PAGE_SIZE),
                                          lambda n, pp, h=h, j=j: (pp[(n * nh + h) * N_SEL_PAGES + j], h, 0, 0))
    per = lambda w: pl.BlockSpec((1, nh, 1, w), lambda n, pp: (n, 0, 0, 0))
    n_pages = nh * N_SEL_PAGES
    return pl.pallas_call(
        _vpass_kernel,
        grid_spec=pltpu.PrefetchScalarGridSpec(
            num_scalar_prefetch=1,
            grid=(db,),
            in_specs=[per(p_sel.shape[3]), per(LANES), per(ATTN_HEAD_DIM)]
                     + [page_spec(h, j) for h in range(nh) for j in range(N_SEL_PAGES)],
            out_specs=per(ATTN_HEAD_DIM)),
        out_shape=jax.ShapeDtypeStruct((db, nh, 1, ATTN_HEAD_DIM), F32),
        compiler_params=_params(1),
        name="sample_vpass",
    )(phys, p_sel, p_own, v_new, *([cache_vt] * n_pages))


def kernel(x_prompt, x_sample, cache_k, cache_v, state_hgrn, page_table, c_prompt, c_sample, rel_bias, hgrn_lb_logits,
           w_ada, b_ada, norm_mix_w, w_in, hgrn_gnorm_w, w_up_attn, w_up_hgrn, w_out, norm_ffn_w, w_ff1, w_ff2, final_norm_w):
    depth = w_in.shape[0]
    assert depth == 1, "one-layer trunk"
    b, s, d = x_prompt.shape
    db, ds, _ = x_sample.shape
    assert ds == 1
    past = page_table.shape[1] * PAGE_SIZE

    w = w_in[0]
    a0, a1, a2 = ATTN_WIDTH, 2 * ATTN_WIDTH, 3 * ATTN_WIDTH
    h0 = a2 + 4 * HGRN_WIDTH
    wq = w[:, :a0].astype(BF16)
    wh = w[:, a2:h0].astype(BF16)
    wg = w[:, h0:].astype(BF16)
    wt_kv = w[:, a0:a2].T.astype(BF16)
    wt_s = jnp.concatenate([w[:, a0:a2], w[:, :a0], w[:, a2:a2 + 2 * HGRN_WIDTH]], axis=1).T.astype(BF16)
    wua, wuh, wo = w_up_attn[0].astype(BF16), w_up_hgrn[0].astype(BF16), w_out[0].astype(BF16)
    w1, w2 = w_ff1[0].astype(BF16), w_ff2[0].astype(BF16)
    fn = final_norm_w.reshape(1, d)

    ada = _ada(jnp.concatenate([c_prompt, c_sample], axis=0), w_ada[0], b_ada)
    ada_p = ada[:b].reshape(b, 1, N_ADA * d)
    ada_s = ada[b:].reshape(1, db, N_ADA * d)
    t_own, t_prev, t_samp = _bias_tables(rel_bias)

    xs = x_sample.reshape(1, db, d)
    qa_s, hg_s, gt_s, kt_s, vt_s, qt_s, qht_s, fht_s = _inproj(xs, ada_s, norm_mix_w, wq, wh, wg, wt_s, db)
    qa, hg, gt, kt, vt = _inproj(x_prompt, ada_p, norm_mix_w, wq, wh, wg, wt_kv, ROW_TILE)
    cache_kt = jnp.transpose(cache_k[0], (0, 2, 3, 1))
    cache_vt = jnp.transpose(cache_v[0], (0, 2, 3, 1))

    o_attn, p_sel, idx, p_own = _attention(qa, kt, vt, t_own, t_prev, page_table, qt_s, kt_s, t_samp,
                                           cache_kt.reshape(-1, ATTN_WIDTH, PAGE_SIZE))
    o_hgrn, state_p = _hgrn_prompt(hg, hgrn_lb_logits, hgrn_gnorm_w)
    y_prompt = _post(x_prompt, o_attn, o_hgrn, gt, ada_p, wua, wuh, wo, norm_ffn_w, w1, w2, fn, ROW_TILE)

    ppb = MOBA_BLOCK // PAGE_SIZE
    logical = idx[:, :, :MOBA_TOPK, None] * ppb + jnp.arange(ppb, dtype=I32)
    phys = jnp.take_along_axis(page_table, logical.reshape(db, -1), axis=1).reshape(-1)
    v_new = jnp.transpose(vt_s[0], (2, 0, 1))[:, :, None, :]
    o_attn_s = _vpass(phys, p_sel[:, :, None, :], p_own[:, :, None, :], v_new, cache_vt).reshape(1, db, ATTN_WIDTH)
    xt = jnp.concatenate([qht_s, fht_s], axis=1)
    o_hgrn_s, state_s = _hgrn_sample(xt, hg_s.reshape(db, 1, -1), hgrn_lb_logits.T, hgrn_gnorm_w, state_hgrn[0])
    y_sample = _post(xs, o_attn_s, o_hgrn_s.reshape(1, db, HGRN_WIDTH), gt_s, ada_s, wua, wuh, wo, norm_ffn_w, w1, w2, fn, db)

    to_rows = lambda t: jnp.transpose(t, (0, 3, 1, 2))[None]
    return (y_prompt, y_sample.reshape(db, 1, d), to_rows(kt), to_rows(vt), state_p[None],
            to_rows(kt_s).reshape(1, db, 1, ATTN_HEADS, ATTN_HEAD_DIM), to_rows(vt_s).reshape(1, db, 1, ATTN_HEADS, ATTN_HEAD_DIM),
            state_s[None])
```

```python
import functools
import math

import numpy as np
import jax
import jax.numpy as jnp
from jax import lax
from jax.experimental import pallas as pl
from jax.experimental.pallas import tpu as pltpu

F32, BF16, I32 = jnp.float32, jnp.bfloat16, jnp.int32
HIGHEST = lax.Precision.HIGHEST

ATTN_HEADS = 8
ATTN_HEAD_DIM = 64
ATTN_WIDTH = ATTN_HEADS * ATTN_HEAD_DIM
MOBA_BLOCK = 256
MOBA_TOPK = 3
PAGE_SIZE = 128
N_BUCKETS = 32
MAX_DISTANCE = 128
HGRN_HEADS = 4
HGRN_DIM = 128
HGRN_WIDTH = HGRN_HEADS * HGRN_DIM
N_ADA = 6
EPS = 1e-6
SCALE = ATTN_HEAD_DIM ** -0.5
LOG2E = math.log2(math.e)

LANES = 128
SUBLANES = 8
VMEM_BYTES_V7X = 64 * 1024 * 1024
VMEM_LIMIT = VMEM_BYTES_V7X * 7 // 8

NEG = -1e30
SUB = 16
TILE_UNROLL = 4
ROW_TILE = 256
PAGES_PER_STEP = 16
PAGE_SLOTS = 3
SAMPLES_PER_STEP = 8
PAIR = 2 * ATTN_HEAD_DIM
PAIR_SLOTS = 8
V_PAD_ROWS = 16


def _params(n_grid):
    return pltpu.CompilerParams(dimension_semantics=("arbitrary",) * n_grid, vmem_limit_bytes=VMEM_LIMIT)


def _const_spec(shape):
    nd = len(shape)
    return pl.BlockSpec(shape, lambda *_: (0,) * nd, pipeline_mode=pl.Buffered(1))


def _sigmoid(x):
    return jax.nn.sigmoid(x)


def _rms_mod(x, w, shift, scale):
    var = jnp.mean(x * x, axis=-1, keepdims=True)
    return (x * lax.rsqrt(var + EPS) * w) * (1.0 + scale) + shift


def _ada_kernel(c_ref, w_ref, b_ref, o_ref):
    c = c_ref[...]
    o_ref[...] = jnp.dot(c * _sigmoid(c), w_ref[...], preferred_element_type=F32, precision=HIGHEST) + b_ref[...]


def _ada(c_all, w_ada, b_ada):
    n, d = c_all.shape
    width = w_ada.shape[1]
    tn = 1536
    assert width % tn == 0
    return pl.pallas_call(
        _ada_kernel,
        grid=(width // tn,),
        in_specs=[pl.BlockSpec((n, d), lambda j: (0, 0)),
                  pl.BlockSpec((d, tn), lambda j: (0, j)),
                  pl.BlockSpec((1, tn), lambda j: (0, j))],
        out_specs=pl.BlockSpec((n, tn), lambda j: (0, j)),
        out_shape=jax.ShapeDtypeStruct((n, width), F32),
        compiler_params=_params(1),
        name="ada",
    )(c_all, w_ada, b_ada)


def _bucket_np(dist):
    n = np.maximum(dist, 0)
    max_exact = N_BUCKETS // 2
    nf = np.maximum(n, max_exact).astype(np.float32)
    large = max_exact + (np.log(nf / np.float32(max_exact)) / np.float32(math.log(MAX_DISTANCE / max_exact))
                         * np.float32(N_BUCKETS - max_exact)).astype(np.int32)
    large = np.minimum(large, N_BUCKETS - 1)
    return np.where(n < max_exact, n, large).astype(np.int32)


def _bucket_tables():
    i = np.arange(MOBA_BLOCK)[:, None]
    j = np.arange(MOBA_BLOCK)[None, :]
    own = np.where(j <= i, _bucket_np(i - j), -1).astype(np.int32)
    prev = _bucket_np(MOBA_BLOCK + i - j)
    assert int(_bucket_np(np.array([MOBA_BLOCK + 1]))[0]) == N_BUCKETS - 1
    samp = np.full((1, MOBA_BLOCK + LANES), N_BUCKETS - 1, np.int32)
    samp[0, :MOBA_BLOCK] = _bucket_np(MOBA_BLOCK - np.arange(MOBA_BLOCK))
    samp[0, MOBA_BLOCK] = 0
    return own, prev, samp


def _bias_kernel(rb_ref, bo_ref, bp_ref, bs_ref, to_ref, tp_ref, ts_ref):
    h = pl.program_id(0)
    far = rb_ref[N_BUCKETS - 1, h]

    def table(bucket):
        acc = jnp.where(bucket < 0, NEG, 0.0).astype(F32)
        for k in range(N_BUCKETS - 1):
            acc = jnp.where(bucket == k, rb_ref[k, h] - far, acc)
        return acc

    to_ref[0] = table(bo_ref[...]) * LOG2E
    tp_ref[0] = table(bp_ref[...]) * LOG2E
    ts_ref[0] = table(bs_ref[...])


def _bias_tables(rel_bias):
    own, prev, samp = _bucket_tables()
    nh = rel_bias.shape[1]
    b = MOBA_BLOCK
    return pl.pallas_call(
        _bias_kernel,
        grid=(nh,),
        in_specs=[pl.BlockSpec(memory_space=pltpu.SMEM),
                  pl.BlockSpec((b, b), lambda h: (0, 0)),
                  pl.BlockSpec((b, b), lambda h: (0, 0)),
                  pl.BlockSpec((1, b + LANES), lambda h: (0, 0))],
        out_specs=[pl.BlockSpec((1, b, b), lambda h: (h, 0, 0)),
                   pl.BlockSpec((1, b, b), lambda h: (h, 0, 0)),
                   pl.BlockSpec((1, 1, b + LANES), lambda h: (h, 0, 0))],
        out_shape=[jax.ShapeDtypeStruct((nh, b, b), F32),
                   jax.ShapeDtypeStruct((nh, b, b), F32),
                   jax.ShapeDtypeStruct((nh, 1, b + LANES), F32)],
        compiler_params=_params(1),
        name="bias_tables",
    )(rel_bias, jnp.asarray(own.T), jnp.asarray(prev.T), jnp.asarray(samp))


def _inproj_kernel(n_t, x_ref, ada_ref, nw_ref, wq_ref, wh_ref, wg_ref, wt_ref, qa_ref, hg_ref, gt_ref, *t_refs):
    d = x_ref.shape[-1]
    ada = ada_ref[0]
    h = _rms_mod(x_ref[0], nw_ref[...], ada[:, 0:d], ada[:, d:2 * d]).astype(BF16)
    qa_ref[0] = jnp.dot(h, wq_ref[...], preferred_element_type=F32)
    hg_ref[0] = jnp.dot(h, wh_ref[...], preferred_element_type=F32)
    gt_ref[0] = jnp.dot(h, wg_ref[...], preferred_element_type=F32)
    t = lax.dot_general(wt_ref[...], h, (((1,), (1,)), ((), ())), preferred_element_type=F32)
    for i in range(n_t):
        ref = t_refs[i]
        ref[0] = t[i * ATTN_WIDTH:(i + 1) * ATTN_WIDTH].reshape(ref.shape[1:])


def _inproj(x, ada, nw, wq, wh, wg, wt, tm):
    b, s, d = x.shape
    r = ada.shape[1]
    n_t = wt.shape[0] // ATTN_WIDTH
    assert s % tm == 0 and r in (1, s) and (r == 1 or tm == s)
    t_shapes, t_specs = [], []
    for i in range(n_t):
        if i < 2:
            t_shapes.append(jax.ShapeDtypeStruct((b, ATTN_HEADS, ATTN_HEAD_DIM, s), F32))
            t_specs.append(pl.BlockSpec((1, ATTN_HEADS, ATTN_HEAD_DIM, tm), lambda i_, j_: (i_, 0, 0, j_)))
        else:
            t_shapes.append(jax.ShapeDtypeStruct((b, ATTN_WIDTH, s), F32))
            t_specs.append(pl.BlockSpec((1, ATTN_WIDTH, tm), lambda i_, j_: (i_, 0, j_)))
    row = lambda w: pl.BlockSpec((1, tm, w), lambda i_, j_: (i_, j_, 0))
    return pl.pallas_call(
        functools.partial(_inproj_kernel, n_t),
        grid=(b, s // tm),
        in_specs=[row(d),
                  pl.BlockSpec((1, r if r == 1 else tm, N_ADA * d), lambda i_, j_: (i_, 0, 0)),
                  _const_spec(nw.shape), _const_spec(wq.shape), _const_spec(wh.shape), _const_spec(wg.shape),
                  _const_spec(wt.shape)],
        out_specs=[row(wq.shape[1]), row(wh.shape[1]), row(wg.shape[1])] + t_specs,
        out_shape=[jax.ShapeDtypeStruct((b, s, wq.shape[1]), F32),
                   jax.ShapeDtypeStruct((b, s, wh.shape[1]), F32),
                   jax.ShapeDtypeStruct((b, s, wg.shape[1]), F32)] + t_shapes,
        compiler_params=_params(2),
        name="inproj",
    )(x, ada, nw, wq, wh, wg, wt)


def _attn_consts(s):
    nb = s // MOBA_BLOCK
    key_blk = np.arange(s) // MOBA_BLOCK
    gm = np.zeros((LANES, LANES), np.float32)
    for b in range(nb):
        for b2 in range(nb):
            gm[b, b * PAIR_SLOTS + b2] = 1.0
    er = np.zeros((s, LANES), np.float32)
    er[np.arange(s), key_blk] = 1.0
    return jnp.asarray(gm, BF16), jnp.asarray(er, BF16)


def _attn_prompt_kernel(spp, pt_ref, q_ref, kt_ref, vt_ref, to_ref, tp_ref, gm_ref, er_ref, qts_ref, ktn_ref, ts_ref, cache_ref,
                        o_ref, ps_ref, idx_ref, po_ref, kaug_ref, vaug_ref, qb_ref, sc_ref, pages_ref, sem):
    s_len = q_ref.shape[1]
    nb = s_len // MOBA_BLOCK
    blk = MOBA_BLOCK

    step = pl.program_id(0) * pl.num_programs(1) + pl.program_id(1)
    n_steps = pl.num_programs(0) * pl.num_programs(1)
    n_batches = pt_ref.shape[1] // PAGES_PER_STEP
    work = [(u, g) for u in range(spp) for g in range(n_batches)]
    n_ahead = min(PAGE_SLOTS - 1, len(work))

    def batch_copies(w, of_step):
        u, g = work[w]
        n = of_step * spp + u
        slot = w % PAGE_SLOTS
        return [pltpu.make_async_copy(cache_ref.at[pt_ref[n, g * PAGES_PER_STEP + j]], pages_ref.at[slot, j], sem.at[slot])
                for j in range(PAGES_PER_STEP)]

    def start_batch(w, of_step):
        for c in batch_copies(w, of_step):
            c.start()

    def finish_batch(w):
        u, g = work[w]
        n = step * spp + u
        for c in batch_copies(w, step):
            c.wait()
        if w + n_ahead < len(work):
            start_batch(w + n_ahead, step)
        if g == 0:
            _sample_query(qts_ref, n, qb_ref)
        _page_scores(pages_ref.at[w % PAGE_SLOTS], qb_ref, sc_ref, g * PAGES_PER_STEP)
        if g == n_batches - 1:
            _select_blocks(sc_ref[...], n, qts_ref, ktn_ref, ts_ref, ps_ref, idx_ref, po_ref, u)

    order = list(range(nb - 1, -1, -1))
    due = [-1] + [min(nb - 1, ((w - 1) * max(1, nb // 2)) // len(work)) for w in range(1, len(work))]

    @pl.when(step == 0)
    def _():
        for w in range(n_ahead):
            start_batch(w, step)
    k_rows = kt_ref[0].reshape(PAIR, s_len).T
    kaug_ref[:, 0:PAIR] = k_rows.astype(BF16)
    kaug_ref[:, PAIR:2 * PAIR] = er_ref[...]
    vaug_ref[0:PAIR, :] = vt_ref[0].reshape(PAIR, s_len).astype(BF16)
    vaug_ref[PAIR:, :] = jnp.where(lax.broadcasted_iota(I32, (V_PAD_ROWS, s_len), 0) == 0, 1.0, 0.0).astype(BF16)
    q_t = q_ref[0].T * (SCALE * LOG2E)
    dim_row = lax.broadcasted_iota(I32, (PAIR, blk), 0)
    slot = lax.broadcasted_iota(I32, (LANES, 2 * blk), 0)
    t_own = jnp.concatenate([to_ref[0], to_ref[1]], axis=1)
    t_prev = jnp.concatenate([tp_ref[0], tp_ref[1]], axis=1)
    for w in range(len(work)):
        if due[w] < 0:
            finish_batch(w)

    q_hi, q_lo = [], []
    for j in range(nb):
        q2 = q_t[:, j * blk:(j + 1) * blk]
        qs = jnp.concatenate([jnp.where(dim_row < ATTN_HEAD_DIM, q2, 0.0), jnp.where(dim_row >= ATTN_HEAD_DIM, q2, 0.0)], axis=1)
        q_hi.append(qs.astype(BF16))
        q_lo.append((qs - q_hi[j].astype(F32)).astype(BF16) if j > MOBA_TOPK else None)

    mask_rows = [None] * nb
    if nb > MOBA_TOPK + 1:
        slot_k = lax.broadcasted_iota(I32, (LANES, PAIR), 0)
        dmean = jnp.zeros((LANES, PAIR), F32)
        for b in range(nb - 1):
            mb = jnp.sum(k_rows[b * blk:(b + 1) * blk], axis=0, keepdims=True) * (1.0 / blk)
            dmean = dmean + jnp.where(slot_k % PAIR_SLOTS == b, mb, 0.0) - jnp.where(slot_k // PAIR_SLOTS == b, mb, 0.0)
        d_hi = dmean.astype(BF16)
        d_lo = (dmean - d_hi.astype(F32)).astype(BF16)
        d_stack = jnp.concatenate([d_hi, d_hi, d_lo], axis=1)
        blk_b = slot // PAIR_SLOTS
        blk_b2 = slot % PAIR_SLOTS
        diffs = {j: jnp.dot(d_stack, jnp.concatenate([q_hi[j], q_lo[j], q_hi[j]], axis=0), preferred_element_type=F32)
                 for j in range(MOBA_TOPK + 1, nb)}
        ranks = {}
        for j, diff in diffs.items():
            beats = ((diff > 0.0) | ((diff == 0.0) & (blk_b2 < blk_b))) & (blk_b2 < j) & (blk_b < j)
            ranks[j] = jnp.dot(gm_ref[...], jnp.where(beats, 1.0, 0.0).astype(BF16), preferred_element_type=F32)
        for j, rank in ranks.items():
            keep = (slot == j) | ((slot < j) & (rank < MOBA_TOPK)) | (slot >= PAIR_SLOTS)
            mask_rows[j] = jnp.where(keep, 0.0, NEG).astype(BF16)

    def scores(j):
        n_keys = (j + 1) * blk
        if mask_rows[j] is None:
            return jnp.dot(kaug_ref[0:n_keys, 0:PAIR], q_hi[j], preferred_element_type=F32)
        return jnp.dot(kaug_ref[0:n_keys, :], jnp.concatenate([q_hi[j], mask_rows[j]], axis=0), preferred_element_type=F32)

    sc_next = scores(order[0])
    for pos, j in enumerate(order):
        n_keys = (j + 1) * blk
        sc = sc_next
        if pos + 1 < nb:
            sc_next = scores(order[pos + 1])
        pieces = []
        if j >= 2:
            pieces.append(sc[:n_keys - 2 * blk])
        if j >= 1:
            pieces.append(sc[n_keys - 2 * blk:n_keys - blk] + t_prev)
        pieces.append(sc[n_keys - blk:] + t_own)
        sc = jnp.concatenate(pieces, axis=0) if len(pieces) > 1 else pieces[0]
        m = jnp.max(sc, axis=0, keepdims=True)
        p = jnp.exp2(sc - m).astype(BF16)
        o2 = jnp.dot(vaug_ref[:, 0:n_keys], p, preferred_element_type=F32)
        o2 = o2[0:PAIR] * (1.0 / o2[PAIR:PAIR + 1])
        heads = jnp.concatenate([o2[:ATTN_HEAD_DIM, :blk], o2[ATTN_HEAD_DIM:, blk:]], axis=0)
        o_ref[0, j * blk:(j + 1) * blk, :] = heads.T
        for w in range(len(work)):
            if due[w] == pos:
                finish_batch(w)

    @pl.when(step + 1 < n_steps)
    def _():
        for w in range(n_ahead):
            start_batch(w, step + 1)


def _attention(qa, kt, vt, t_own, t_prev, page_table, qt_s, kt_new, t_samp, cache_kt):
    b, s, _ = qa.shape
    db, n_pages = page_table.shape
    n_pairs = ATTN_HEADS // 2
    past = n_pages * PAGE_SIZE
    assert s % MOBA_BLOCK == 0 and s // MOBA_BLOCK <= PAIR_SLOTS
    assert db % (b * n_pairs) == 0 and n_pages % PAGES_PER_STEP == 0
    assert past % MOBA_BLOCK == 0 and MOBA_TOPK <= past // MOBA_BLOCK <= LANES
    spp = db // (b * n_pairs)
    gm, er = _attn_consts(s)
    wsel = MOBA_TOPK * MOBA_BLOCK
    const = lambda shape: pl.BlockSpec(shape, lambda i, p, pt: (0,) * len(shape), pipeline_mode=pl.Buffered(1))
    per_sample = lambda w: pl.BlockSpec((spp, ATTN_HEADS, w), lambda i, p, pt: (i * n_pairs + p, 0, 0))
    return pl.pallas_call(
        functools.partial(_attn_prompt_kernel, spp),
        grid_spec=pltpu.PrefetchScalarGridSpec(
            num_scalar_prefetch=1,
            grid=(b, n_pairs),
            in_specs=[pl.BlockSpec((1, s, PAIR), lambda i, p, pt: (i, 0, p)),
                      pl.BlockSpec((1, 2, ATTN_HEAD_DIM, s), lambda i, p, pt: (i, p, 0, 0)),
                      pl.BlockSpec((1, 2, ATTN_HEAD_DIM, s), lambda i, p, pt: (i, p, 0, 0)),
                      pl.BlockSpec((2, MOBA_BLOCK, MOBA_BLOCK), lambda i, p, pt: (p, 0, 0)),
                      pl.BlockSpec((2, MOBA_BLOCK, MOBA_BLOCK), lambda i, p, pt: (p, 0, 0)),
                      const(gm.shape), const(er.shape), const(qt_s.shape), const(kt_new.shape), const(t_samp.shape),
                      pl.BlockSpec(memory_space=pl.ANY)],
            out_specs=[pl.BlockSpec((1, s, PAIR), lambda i, p, pt: (i, 0, p)),
                       per_sample(wsel), per_sample(LANES), per_sample(LANES)],
            scratch_shapes=[pltpu.VMEM((s, 2 * PAIR), BF16), pltpu.VMEM((PAIR + V_PAD_ROWS, s), BF16),
                            pltpu.VMEM((ATTN_WIDTH, PAGE_SIZE), F32), pltpu.VMEM((ATTN_HEADS, past), F32),
                            pltpu.VMEM((PAGE_SLOTS, PAGES_PER_STEP, ATTN_WIDTH, PAGE_SIZE), F32),
                            pltpu.SemaphoreType.DMA((PAGE_SLOTS,))]),
        out_shape=[jax.ShapeDtypeStruct((b, s, ATTN_WIDTH), F32),
                   jax.ShapeDtypeStruct((db, ATTN_HEADS, wsel), F32),
                   jax.ShapeDtypeStruct((db, ATTN_HEADS, LANES), I32),
                   jax.ShapeDtypeStruct((db, ATTN_HEADS, LANES), F32)],
        compiler_params=_params(2),
        name="attention",
    )(page_table, qa, kt, vt, t_own, t_prev, gm, er, qt_s, kt_new, t_samp, cache_kt)


def _lower_bound(logits, axis):
    m = jnp.max(logits, axis=axis, keepdims=True)
    e = jnp.exp(logits - m)
    first = e[0:1] if axis == 0 else e[:, 0:1]
    return first / jnp.sum(e, axis=axis, keepdims=True)


def _gated_out(o, gw, og):
    var = jnp.mean(o * o, axis=-1, keepdims=True)
    return (o * lax.rsqrt(var + EPS) * gw) * (og * _sigmoid(og))


def _cumsum_rows(tri, x):
    hi = x.astype(BF16)
    rest = x - hi.astype(F32)
    mid = rest.astype(BF16)
    lo = (rest - mid.astype(F32)).astype(BF16)
    n = x.shape[1]
    y = jnp.dot(tri, jnp.concatenate([hi, mid, lo], axis=1), preferred_element_type=F32)
    return y[:, :n] + y[:, n:2 * n] + y[:, 2 * n:]


def _hgrn_prompt_kernel(spp, pp_ref, q_ref, f_ref, v_ref, og_ref, lbl_ref, gw_ref, tri_ref, ps_ref, po_ref, vn_ref, cache_ref,
                        o_ref, s_ref, oa_ref, shift_s, vrow_s, vpages_ref, vsem):
    step = pl.program_id(0) * pl.num_programs(1) + pl.program_id(1)
    n_steps = pl.num_programs(0) * pl.num_programs(1)

    @pl.when(step == 0)
    def _():
        for u in range(spp):
            for c in _value_copies(pp_ref, cache_ref, vpages_ref, vsem, u, u):
                c.start()

    for u in range(spp):
        for c in _value_copies(pp_ref, cache_ref, vpages_ref, vsem, step * spp + u, u):
            c.wait()
        _weighted_values(vpages_ref, ps_ref, po_ref, vn_ref, oa_ref, u)

    t_len = q_ref.shape[1]
    n_sub = LANES // SUB
    half = SUB // 2
    dim = HGRN_DIM
    lb = _lower_bound(lbl_ref[...], 0)
    gw = gw_ref[...]
    tri = tri_ref[...]
    half_row = lax.broadcasted_iota(I32, (n_sub, half, dim), 1)
    trans_b = (((1,), (1,)), ((), ()))

    def prepare(t):
        rows = pl.ds(pl.multiple_of(t * LANES, LANES), LANES)
        qr = q_ref[0, rows, :]
        q = qr * _sigmoid(qr)
        f = lb + (1.0 - lb) * _sigmoid(f_ref[0, rows, :])
        v = v_ref[0, rows, :]
        cum = _cumsum_rows(tri, jnp.log2(f))
        return dict(rows=rows, q=q, kk=1.0 - f, v=v, vb=v.astype(BF16), cum=cum)

    def split(p):
        c3 = p["cum"].reshape(n_sub, SUB, dim)
        start = jnp.concatenate([jnp.zeros((1, 1, dim), F32), c3[:n_sub - 1, SUB - 1:SUB, :]], axis=0)
        p.update(c3=c3, start=start, b3=c3 - start, c_end=c3[n_sub - 1, SUB - 1:SUB, :],
                 q3=p["q"].reshape(n_sub, SUB, dim), k3=p["kk"].reshape(n_sub, SUB, dim))

    def first_products(p):
        p["upd"] = lax.dot_general(p["vb"], (p["kk"] * jnp.exp2(p["c_end"] - p["cum"])).astype(BF16),
                                   (((0,), (0,)), ((), ())), preferred_element_type=F32)
        qt3 = (p["q3"] * jnp.exp2(p["b3"])).astype(BF16)
        p["scores"] = []
        for i in range(1, n_sub):
            kt = (p["k3"][:i] * jnp.exp2(p["start"][i:i + 1] - p["c3"][:i])).reshape(i * SUB, dim).astype(BF16)
            p["scores"].append(lax.dot_general(qt3[i], kt, trans_b, preferred_element_type=F32).astype(BF16))

    def within_sub_chunks(p, slot):
        b3, q3 = p["b3"], p["q3"]
        shift_s[slot] = (b3 - jnp.log2(p["k3"])).reshape(LANES, dim)
        vrow_s[slot] = p["v"]
        key_row = lambda ref, s: jnp.concatenate(
            [jnp.broadcast_to(ref[slot, c * SUB + s:c * SUB + s + 1, :], (1, half, dim)) for c in range(n_sub)], axis=0)
        b_lo, b_hi, q_lo, q_hi = b3[:, :half], b3[:, half:], q3[:, :half], q3[:, half:]
        od_lo = jnp.zeros((n_sub, half, dim), F32)
        od_hi = jnp.zeros((n_sub, half, dim), F32)
        for s in range(SUB):
            bs, vs = key_row(shift_s, s), key_row(vrow_s, s)
            w_hi = jnp.sum(q_hi * jnp.exp2(b_hi - bs), axis=-1, keepdims=True)
            if s < half:
                w_lo = jnp.sum(q_lo * jnp.exp2(b_lo - bs), axis=-1, keepdims=True)
                od_lo = od_lo + jnp.where(half_row >= s, w_lo, 0.0) * vs
                od_hi = od_hi + w_hi * vs
            else:
                od_hi = od_hi + jnp.where(half_row >= s - half, w_hi, 0.0) * vs
        p["o_sub"] = jnp.concatenate([od_lo, od_hi], axis=1).reshape(LANES, dim)

    def tiles(g, st):
        group = [prepare(g * TILE_UNROLL + u) for u in range(TILE_UNROLL)]
        for p in group:
            split(p)
            first_products(p)
        for p in group:
            p["o_state"] = lax.dot_general((p["q"] * jnp.exp2(p["cum"])).astype(BF16), st.astype(BF16), trans_b,
                                           preferred_element_type=F32)
            st = st * jnp.exp2(p["c_end"]) + p["upd"]
        for u, p in enumerate(group):
            within_sub_chunks(p, u)
        for p in group:
            cross = [jnp.dot(a, p["vb"][0:(i + 1) * SUB], preferred_element_type=F32) for i, a in enumerate(p["scores"])]
            o = p["o_sub"] + p["o_state"] + jnp.concatenate([jnp.zeros((SUB, dim), F32)] + cross, axis=0)
            o_ref[0, p["rows"], :] = _gated_out(o, gw, og_ref[0, p["rows"], :])
        return st

    st = lax.fori_loop(0, t_len // (LANES * TILE_UNROLL), tiles, jnp.zeros((dim, dim), F32))
    s_ref[0, 0] = st.T

    @pl.when(step + 1 < n_steps)
    def _():
        for u in range(spp):
            for c in _value_copies(pp_ref, cache_ref, vpages_ref, vsem, (step + 1) * spp + u, u):
                c.start()


def _hgrn_tri():
    r = np.arange(LANES)
    return jnp.asarray((r[None, :] <= r[:, None]).astype(np.float32), BF16)


def _hgrn_prompt(hg, lb_logits, gw, phys, p_sel, p_own, v_new, cache_vt):
    b, s, _ = hg.shape
    db = p_sel.shape[0]
    assert s % (LANES * TILE_UNROLL) == 0 and db % (b * HGRN_HEADS) == 0
    spp = db // (b * HGRN_HEADS)
    col = lambda part: pl.BlockSpec((1, s, HGRN_DIM), lambda i, h, pp, part=part: (i, 0, part * HGRN_HEADS + h))
    const = lambda shape: pl.BlockSpec(shape, lambda i, h, pp: (0,) * len(shape), pipeline_mode=pl.Buffered(1))
    per_sample = lambda w: pl.BlockSpec((spp, ATTN_HEADS, 1, w), lambda i, h, pp: (i * HGRN_HEADS + h, 0, 0, 0))
    return pl.pallas_call(
        functools.partial(_hgrn_prompt_kernel, spp),
        grid_spec=pltpu.PrefetchScalarGridSpec(
            num_scalar_prefetch=1,
            grid=(b, HGRN_HEADS),
            in_specs=[col(0), col(1), col(2), col(3),
                      pl.BlockSpec((lb_logits.shape[0], HGRN_DIM), lambda i, h, pp: (0, h)),
                      const(gw.shape), const((LANES, LANES)),
                      per_sample(p_sel.shape[3]), per_sample(LANES), per_sample(ATTN_HEAD_DIM),
                      pl.BlockSpec(memory_space=pl.ANY)],
            out_specs=[pl.BlockSpec((1, s, HGRN_DIM), lambda i, h, pp: (i, 0, h)),
                       pl.BlockSpec((1, 1, HGRN_DIM, HGRN_DIM), lambda i, h, pp: (i, h, 0, 0)),
                       per_sample(ATTN_HEAD_DIM)],
            scratch_shapes=[pltpu.VMEM((TILE_UNROLL, LANES, HGRN_DIM), F32), pltpu.VMEM((TILE_UNROLL, LANES, HGRN_DIM), F32),
                            pltpu.VMEM((spp * N_SEL_TILES, ATTN_HEAD_DIM, PAGE_SIZE), F32),
                            pltpu.SemaphoreType.DMA((spp,))]),
        out_shape=[jax.ShapeDtypeStruct((b, s, HGRN_WIDTH), F32),
                   jax.ShapeDtypeStruct((b, HGRN_HEADS, HGRN_DIM, HGRN_DIM), F32),
                   jax.ShapeDtypeStruct((db, ATTN_HEADS, 1, ATTN_HEAD_DIM), F32)],
        compiler_params=_params(2),
        name="hgrn_prompt",
    )(phys, hg, hg, hg, hg, lb_logits, gw, _hgrn_tri(), p_sel, p_own, v_new, cache_vt)


def _hgrn_sample_kernel(xt_ref, hg_ref, lblt_ref, gw_ref, st_ref, o_ref, so_ref):
    width = HGRN_WIDTH
    xt = xt_ref[0]
    lane = lax.broadcasted_iota(I32, xt.shape, 1)
    lb = _lower_bound(lblt_ref[...], 1)
    for u in range(st_ref.shape[0]):
        n = pl.program_id(0) * st_ref.shape[0] + u
        cols = jnp.sum(jnp.where(lane == n, xt, 0.0), axis=1, keepdims=True)
        hg = hg_ref[u]
        for h in range(HGRN_HEADS):
            rows = slice(h * HGRN_DIM, (h + 1) * HGRN_DIM)
            qr = cols[rows]
            q = qr * _sigmoid(qr)
            lbh = lb[rows]
            f = lbh + (1.0 - lbh) * _sigmoid(cols[width + h * HGRN_DIM:width + (h + 1) * HGRN_DIM])
            v = hg[:, 2 * width + h * HGRN_DIM:2 * width + (h + 1) * HGRN_DIM]
            og = hg[:, 3 * width + h * HGRN_DIM:3 * width + (h + 1) * HGRN_DIM]
            s_new = f * st_ref[u, h] + (1.0 - f) * v
            so_ref[u, h] = s_new
            o = jnp.sum(q * s_new, axis=0, keepdims=True)
            o_ref[u, :, rows] = _gated_out(o, gw_ref[...], og)


def _hgrn_sample(xt, hg, lb_logits_t, gw, state):
    db = state.shape[0]
    ns = SAMPLES_PER_STEP
    assert db % ns == 0
    return pl.pallas_call(
        _hgrn_sample_kernel,
        grid=(db // ns,),
        in_specs=[_const_spec(xt.shape),
                  pl.BlockSpec((ns, 1, hg.shape[2]), lambda n: (n, 0, 0)),
                  _const_spec(lb_logits_t.shape), _const_spec(gw.shape),
                  pl.BlockSpec((ns,) + state.shape[1:], lambda n: (n, 0, 0, 0))],
        out_specs=[pl.BlockSpec((ns, 1, HGRN_WIDTH), lambda n: (n, 0, 0)),
                   pl.BlockSpec((ns,) + state.shape[1:], lambda n: (n, 0, 0, 0))],
        out_shape=[jax.ShapeDtypeStruct((db, 1, HGRN_WIDTH), F32),
                   jax.ShapeDtypeStruct(state.shape, F32)],
        compiler_params=_params(1),
        name="hgrn_sample",
    )(xt, hg, lb_logits_t, gw, state)


def _post_kernel(x_ref, oa_ref, oh_ref, gt_ref, ada_ref, wua_ref, wuh_ref, wo_ref, nf_ref, w1_ref, w2_ref, fn_ref, y_ref):
    d = x_ref.shape[-1]
    ada = ada_ref[0]
    g = gt_ref[0]
    ua = jnp.dot(oa_ref[0].astype(BF16), wua_ref[...], preferred_element_type=F32)
    uh = jnp.dot(oh_ref[0].astype(BF16), wuh_ref[...], preferred_element_type=F32)
    merged = _sigmoid(g[:, :d]) * ua + _sigmoid(g[:, d:]) * uh
    x1 = x_ref[0] + ada[:, 2 * d:3 * d] * jnp.dot(merged.astype(BF16), wo_ref[...], preferred_element_type=F32)
    h2 = _rms_mod(x1, nf_ref[...], ada[:, 3 * d:4 * d], ada[:, 4 * d:5 * d]).astype(BF16)
    a = jnp.maximum(jnp.dot(h2, w1_ref[...], preferred_element_type=F32), 0.0)
    x2 = x1 + ada[:, 5 * d:6 * d] * jnp.dot((a * a).astype(BF16), w2_ref[...], preferred_element_type=F32)
    var = jnp.mean(x2 * x2, axis=-1, keepdims=True)
    y_ref[0] = x2 * lax.rsqrt(var + EPS) * fn_ref[...]


def _post(x, oa, oh, gt, ada, wua, wuh, wo, nf, w1, w2, fn, tm):
    b, s, d = x.shape
    r = ada.shape[1]
    assert s % tm == 0 and r in (1, s) and (r == 1 or tm == s)
    row = lambda w: pl.BlockSpec((1, tm, w), lambda i, j: (i, j, 0))
    return pl.pallas_call(
        _post_kernel,
        grid=(b, s // tm),
        in_specs=[row(d), row(oa.shape[2]), row(oh.shape[2]), row(gt.shape[2]),
                  pl.BlockSpec((1, r if r == 1 else tm, N_ADA * d), lambda i, j: (i, 0, 0)),
                  _const_spec(wua.shape), _const_spec(wuh.shape), _const_spec(wo.shape), _const_spec(nf.shape),
                  _const_spec(w1.shape), _const_spec(w2.shape), _const_spec(fn.shape)],
        out_specs=row(d),
        out_shape=jax.ShapeDtypeStruct((b, s, d), F32),
        compiler_params=_params(2),
        name="post",
    )(x, oa, oh, gt, ada, wua, wuh, wo, nf, w1, w2, fn)


def _sample_query(qt_ref, n, qb_ref):
    qt = qt_ref[0]
    pick = lax.broadcasted_iota(I32, qt.shape, 1) == n
    qb_ref[...] = jnp.broadcast_to(jnp.sum(jnp.where(pick, qt, 0.0), axis=1, keepdims=True), qb_ref.shape)


def _page_scores(pages_ref, qb_ref, sc_ref, first_page):
    for h in range(ATTN_HEADS):
        rows = slice(h * ATTN_HEAD_DIM, (h + 1) * ATTN_HEAD_DIM)
        qh = qb_ref[rows, :]
        for j in range(PAGES_PER_STEP):
            lanes = slice((first_page + j) * PAGE_SIZE, (first_page + j + 1) * PAGE_SIZE)
            sc_ref[h:h + 1, lanes] = jnp.sum(pages_ref[j, rows, :] * qh, axis=0, keepdims=True)


def _select_blocks(sc, n, qt_ref, kt_ref, ts_ref, ps_ref, idx_ref, po_ref, u):
    n_blk = sc.shape[1] // MOBA_BLOCK
    lane = lax.broadcasted_iota(I32, (ATTN_HEADS, LANES), 1).astype(F32)
    bs = jnp.full((ATTN_HEADS, LANES), -jnp.inf, F32)
    for b in range(n_blk):
        tot = jnp.sum(sc[:, b * MOBA_BLOCK:(b + 1) * MOBA_BLOCK], axis=1, keepdims=True)
        bs = jnp.where(lane == float(b), tot * (1.0 / MOBA_BLOCK), bs)
    picks = []
    for _ in range(MOBA_TOPK):
        m = jnp.max(bs, axis=1, keepdims=True)
        i = jnp.min(jnp.where(bs == m, lane, float(LANES)), axis=1, keepdims=True)
        picks.append(i)
        bs = jnp.where(lane == i, -jnp.inf, bs)
    ts = ts_ref[...].reshape(ATTN_HEADS, MOBA_BLOCK + LANES)
    logits = []
    for i in picks:
        blk = jnp.zeros((ATTN_HEADS, MOBA_BLOCK), F32)
        for b in range(n_blk):
            blk = jnp.where(i == float(b), sc[:, b * MOBA_BLOCK:(b + 1) * MOBA_BLOCK], blk)
        logits.append(blk * SCALE + jnp.where(i == float(n_blk - 1), ts[:, :MOBA_BLOCK], 0.0))
    lg = jnp.concatenate(logits, axis=1)
    prod = qt_ref[0] * kt_ref[0].reshape(ATTN_WIDTH, -1)
    own_all = jnp.sum(prod.reshape(ATTN_HEADS, ATTN_HEAD_DIM, prod.shape[1]), axis=1)
    pick = lax.broadcasted_iota(I32, own_all.shape, 1) == n
    own = jnp.sum(jnp.where(pick, own_all, 0.0), axis=1, keepdims=True) * SCALE + ts[:, MOBA_BLOCK:MOBA_BLOCK + 1]
    m = jnp.maximum(jnp.max(lg, axis=1, keepdims=True), own)
    e = jnp.exp(lg - m)
    eo = jnp.exp(own - m)
    den = jnp.sum(e, axis=1, keepdims=True) + eo
    ps_ref[u] = e / den
    po_ref[u] = jnp.broadcast_to(eo / den, (ATTN_HEADS, LANES))
    idx = jnp.zeros((ATTN_HEADS, LANES), F32)
    for t, i in enumerate(picks):
        idx = jnp.where(lane == float(t), i, idx)
    idx_ref[u] = idx.astype(I32)


N_SEL_PAGES = MOBA_TOPK * (MOBA_BLOCK // PAGE_SIZE)


N_SEL_TILES = ATTN_HEADS * N_SEL_PAGES


def _value_copies(pp_ref, cache_ref, vpages_ref, sem, n, u):
    return [pltpu.make_async_copy(cache_ref.at[pp_ref[(n * ATTN_HEADS + h) * N_SEL_PAGES + j], h],
                                  vpages_ref.at[u * N_SEL_TILES + h * N_SEL_PAGES + j], sem.at[u])
            for h in range(ATTN_HEADS) for j in range(N_SEL_PAGES)]


def _weighted_values(vpages_ref, ps_ref, po_ref, vn_ref, o_ref, u):
    ones = jnp.ones((SUBLANES, PAGE_SIZE), F32)
    for h in range(ATTN_HEADS):
        ps = ps_ref[u, h]
        acc = jnp.zeros((ATTN_HEAD_DIM, PAGE_SIZE), F32)
        for j in range(N_SEL_PAGES):
            acc = acc + vpages_ref[u * N_SEL_TILES + h * N_SEL_PAGES + j] * ps[:, j * PAGE_SIZE:(j + 1) * PAGE_SIZE]
        o = lax.dot_general(ones, acc, (((1,), (1,)), ((), ())), preferred_element_type=F32, precision=HIGHEST)
        o_ref[u, h] = o[0:1] + po_ref[u, h][:, 0:ATTN_HEAD_DIM] * vn_ref[u, h]


def kernel(x_prompt, x_sample, cache_k, cache_v, state_hgrn, page_table, c_prompt, c_sample, rel_bias, hgrn_lb_logits,
           w_ada, b_ada, norm_mix_w, w_in, hgrn_gnorm_w, w_up_attn, w_up_hgrn, w_out, norm_ffn_w, w_ff1, w_ff2, final_norm_w):
    depth = w_in.shape[0]
    assert depth == 1, "one-layer trunk"
    b, s, d = x_prompt.shape
    db, ds, _ = x_sample.shape
    assert ds == 1
    past = page_table.shape[1] * PAGE_SIZE

    w = w_in[0]
    a0, a1, a2 = ATTN_WIDTH, 2 * ATTN_WIDTH, 3 * ATTN_WIDTH
    h0 = a2 + 4 * HGRN_WIDTH
    wq = w[:, :a0].astype(BF16)
    wh = w[:, a2:h0].astype(BF16)
    wg = w[:, h0:].astype(BF16)
    wt_kv = w[:, a0:a2].T.astype(BF16)
    wt_s = jnp.concatenate([w[:, a0:a2], w[:, :a0], w[:, a2:a2 + 2 * HGRN_WIDTH]], axis=1).T.astype(BF16)
    wua, wuh, wo = w_up_attn[0].astype(BF16), w_up_hgrn[0].astype(BF16), w_out[0].astype(BF16)
    w1, w2 = w_ff1[0].astype(BF16), w_ff2[0].astype(BF16)
    fn = final_norm_w.reshape(1, d)

    ada = _ada(jnp.concatenate([c_prompt, c_sample], axis=0), w_ada[0], b_ada)
    ada_p = ada[:b].reshape(b, 1, N_ADA * d)
    ada_s = ada[b:].reshape(1, db, N_ADA * d)
    t_own, t_prev, t_samp = _bias_tables(rel_bias)

    xs = x_sample.reshape(1, db, d)
    qa_s, hg_s, gt_s, kt_s, vt_s, qt_s, qht_s, fht_s = _inproj(xs, ada_s, norm_mix_w, wq, wh, wg, wt_s, db)
    qa, hg, gt, kt, vt = _inproj(x_prompt, ada_p, norm_mix_w, wq, wh, wg, wt_kv, ROW_TILE)
    cache_kt = jnp.transpose(cache_k[0], (0, 2, 3, 1))
    cache_vt = jnp.transpose(cache_v[0], (0, 2, 3, 1))

    o_attn, p_sel, idx, p_own = _attention(qa, kt, vt, t_own, t_prev, page_table, qt_s, kt_s, t_samp,
                                           cache_kt.reshape(-1, ATTN_WIDTH, PAGE_SIZE))
    ppb = MOBA_BLOCK // PAGE_SIZE
    logical = idx[:, :, :MOBA_TOPK, None] * ppb + jnp.arange(ppb, dtype=I32)
    phys = jnp.take_along_axis(page_table, logical.reshape(db, -1), axis=1).reshape(-1)
    v_new = jnp.transpose(vt_s[0], (2, 0, 1))[:, :, None, :]
    o_hgrn, state_p, o_attn_s = _hgrn_prompt(hg, hgrn_lb_logits, hgrn_gnorm_w, phys, p_sel[:, :, None, :],
                                             p_own[:, :, None, :], v_new, cache_vt)
    y_prompt = _post(x_prompt, o_attn, o_hgrn, gt, ada_p, wua, wuh, wo, norm_ffn_w, w1, w2, fn, ROW_TILE)

    o_attn_s = o_attn_s.reshape(1, db, ATTN_WIDTH)
    xt = jnp.concatenate([qht_s, fht_s], axis=1)
    o_hgrn_s, state_s = _hgrn_sample(xt, hg_s.reshape(db, 1, -1), hgrn_lb_logits.T, hgrn_gnorm_w, state_hgrn[0])
    y_sample = _post(xs, o_attn_s, o_hgrn_s.reshape(1, db, HGRN_WIDTH), gt_s, ada_s, wua, wuh, wo, norm_ffn_w, w1, w2, fn, db)

    to_rows = lambda t: jnp.transpose(t, (0, 3, 1, 2))[None]
    return (y_prompt, y_sample.reshape(db, 1, d), to_rows(kt), to_rows(vt), state_p[None],
            to_rows(kt_s).reshape(1, db, 1, ATTN_HEADS, ATTN_HEAD_DIM), to_rows(vt_s).reshape(1, db, 1, ATTN_HEADS, ATTN_HEAD_DIM),
            state_s[None])
```

```python
import functools
import math

import numpy as np
import jax
import jax.numpy as jnp
from jax import lax
from jax.experimental import pallas as pl
from jax.experimental.pallas import tpu as pltpu

F32, BF16, I32 = jnp.float32, jnp.bfloat16, jnp.int32
HIGHEST = lax.Precision.HIGHEST

ATTN_HEADS = 8
ATTN_HEAD_DIM = 64
ATTN_WIDTH = ATTN_HEADS * ATTN_HEAD_DIM
MOBA_BLOCK = 256
MOBA_TOPK = 3
PAGE_SIZE = 128
N_BUCKETS = 32
MAX_DISTANCE = 128
HGRN_HEADS = 4
HGRN_DIM = 128
HGRN_WIDTH = HGRN_HEADS * HGRN_DIM
N_ADA = 6
EPS = 1e-6
SCALE = ATTN_HEAD_DIM ** -0.5
LOG2E = math.log2(math.e)

LANES = 128
SUBLANES = 8
VMEM_BYTES_V7X = 64 * 1024 * 1024
VMEM_LIMIT = VMEM_BYTES_V7X * 7 // 8

NEG = -1e30
SUB = 16
TILE_UNROLL = 4
ROW_TILE = 256
PAGES_PER_STEP = 16
PAGE_SLOTS = 3
SAMPLES_PER_STEP = 8
PAIR = 2 * ATTN_HEAD_DIM
PAIR_SLOTS = 8
V_PAD_ROWS = 16


def _params(n_grid):
    return pltpu.CompilerParams(dimension_semantics=("arbitrary",) * n_grid, vmem_limit_bytes=VMEM_LIMIT)


def _const_spec(shape):
    nd = len(shape)
    return pl.BlockSpec(shape, lambda *_: (0,) * nd, pipeline_mode=pl.Buffered(1))


def _sigmoid(x):
    return jax.nn.sigmoid(x)


def _rms_mod(x, w, shift, scale):
    var = jnp.mean(x * x, axis=-1, keepdims=True)
    return (x * lax.rsqrt(var + EPS) * w) * (1.0 + scale) + shift


def _ada_kernel(c_ref, w_ref, b_ref, o_ref):
    c = c_ref[...]
    o_ref[...] = jnp.dot(c * _sigmoid(c), w_ref[...], preferred_element_type=F32, precision=HIGHEST) + b_ref[...]


def _ada(c_all, w_ada, b_ada):
    n, d = c_all.shape
    width = w_ada.shape[1]
    tn = 1536
    assert width % tn == 0
    return pl.pallas_call(
        _ada_kernel,
        grid=(width // tn,),
        in_specs=[pl.BlockSpec((n, d), lambda j: (0, 0)),
                  pl.BlockSpec((d, tn), lambda j: (0, j)),
                  pl.BlockSpec((1, tn), lambda j: (0, j))],
        out_specs=pl.BlockSpec((n, tn), lambda j: (0, j)),
        out_shape=jax.ShapeDtypeStruct((n, width), F32),
        compiler_params=_params(1),
        name="ada",
    )(c_all, w_ada, b_ada)


def _bucket_np(dist):
    n = np.maximum(dist, 0)
    max_exact = N_BUCKETS // 2
    nf = np.maximum(n, max_exact).astype(np.float32)
    large = max_exact + (np.log(nf / np.float32(max_exact)) / np.float32(math.log(MAX_DISTANCE / max_exact))
                         * np.float32(N_BUCKETS - max_exact)).astype(np.int32)
    large = np.minimum(large, N_BUCKETS - 1)
    return np.where(n < max_exact, n, large).astype(np.int32)


def _bucket_tables():
    i = np.arange(MOBA_BLOCK)[:, None]
    j = np.arange(MOBA_BLOCK)[None, :]
    own = np.where(j <= i, _bucket_np(i - j), -1).astype(np.int32)
    prev = _bucket_np(MOBA_BLOCK + i - j)
    assert int(_bucket_np(np.array([MOBA_BLOCK + 1]))[0]) == N_BUCKETS - 1
    samp = np.full((1, MOBA_BLOCK + LANES), N_BUCKETS - 1, np.int32)
    samp[0, :MOBA_BLOCK] = _bucket_np(MOBA_BLOCK - np.arange(MOBA_BLOCK))
    samp[0, MOBA_BLOCK] = 0
    return own, prev, samp


def _bias_kernel(rb_ref, bo_ref, bp_ref, bs_ref, to_ref, tp_ref, ts_ref):
    h = pl.program_id(0)
    far = rb_ref[N_BUCKETS - 1, h]

    def table(bucket):
        acc = jnp.where(bucket < 0, NEG, 0.0).astype(F32)
        for k in range(N_BUCKETS - 1):
            acc = jnp.where(bucket == k, rb_ref[k, h] - far, acc)
        return acc

    to_ref[0] = table(bo_ref[...]) * LOG2E
    tp_ref[0] = table(bp_ref[...]) * LOG2E
    ts_ref[0] = table(bs_ref[...])


def _bias_tables(rel_bias):
    own, prev, samp = _bucket_tables()
    nh = rel_bias.shape[1]
    b = MOBA_BLOCK
    return pl.pallas_call(
        _bias_kernel,
        grid=(nh,),
        in_specs=[pl.BlockSpec(memory_space=pltpu.SMEM),
                  pl.BlockSpec((b, b), lambda h: (0, 0)),
                  pl.BlockSpec((b, b), lambda h: (0, 0)),
                  pl.BlockSpec((1, b + LANES), lambda h: (0, 0))],
        out_specs=[pl.BlockSpec((1, b, b), lambda h: (h, 0, 0)),
                   pl.BlockSpec((1, b, b), lambda h: (h, 0, 0)),
                   pl.BlockSpec((1, 1, b + LANES), lambda h: (h, 0, 0))],
        out_shape=[jax.ShapeDtypeStruct((nh, b, b), F32),
                   jax.ShapeDtypeStruct((nh, b, b), F32),
                   jax.ShapeDtypeStruct((nh, 1, b + LANES), F32)],
        compiler_params=_params(1),
        name="bias_tables",
    )(rel_bias, jnp.asarray(own.T), jnp.asarray(prev.T), jnp.asarray(samp))


def _inproj_kernel(n_t, x_ref, ada_ref, nw_ref, wq_ref, wh_ref, wg_ref, wt_ref, qa_ref, hg_ref, gt_ref, *t_refs):
    d = x_ref.shape[-1]
    ada = ada_ref[0]
    h = _rms_mod(x_ref[0], nw_ref[...], ada[:, 0:d], ada[:, d:2 * d]).astype(BF16)
    qa_ref[0] = jnp.dot(h, wq_ref[...], preferred_element_type=F32)
    hg_ref[0] = jnp.dot(h, wh_ref[...], preferred_element_type=F32)
    gt_ref[0] = jnp.dot(h, wg_ref[...], preferred_element_type=F32)
    t = lax.dot_general(wt_ref[...], h, (((1,), (1,)), ((), ())), preferred_element_type=F32)
    for i in range(n_t):
        ref = t_refs[i]
        ref[0] = t[i * ATTN_WIDTH:(i + 1) * ATTN_WIDTH].reshape(ref.shape[1:])


def _inproj(x, ada, nw, wq, wh, wg, wt, tm):
    b, s, d = x.shape
    r = ada.shape[1]
    n_t = wt.shape[0] // ATTN_WIDTH
    assert s % tm == 0 and r in (1, s) and (r == 1 or tm == s)
    t_shapes, t_specs = [], []
    for i in range(n_t):
        if i < 2:
            t_shapes.append(jax.ShapeDtypeStruct((b, ATTN_HEADS, ATTN_HEAD_DIM, s), F32))
            t_specs.append(pl.BlockSpec((1, ATTN_HEADS, ATTN_HEAD_DIM, tm), lambda i_, j_: (i_, 0, 0, j_)))
        else:
            t_shapes.append(jax.ShapeDtypeStruct((b, ATTN_WIDTH, s), F32))
            t_specs.append(pl.BlockSpec((1, ATTN_WIDTH, tm), lambda i_, j_: (i_, 0, j_)))
    row = lambda w: pl.BlockSpec((1, tm, w), lambda i_, j_: (i_, j_, 0))
    return pl.pallas_call(
        functools.partial(_inproj_kernel, n_t),
        grid=(b, s // tm),
        in_specs=[row(d),
                  pl.BlockSpec((1, r if r == 1 else tm, N_ADA * d), lambda i_, j_: (i_, 0, 0)),
                  _const_spec(nw.shape), _const_spec(wq.shape), _const_spec(wh.shape), _const_spec(wg.shape),
                  _const_spec(wt.shape)],
        out_specs=[row(wq.shape[1]), row(wh.shape[1]), row(wg.shape[1])] + t_specs,
        out_shape=[jax.ShapeDtypeStruct((b, s, wq.shape[1]), F32),
                   jax.ShapeDtypeStruct((b, s, wh.shape[1]), F32),
                   jax.ShapeDtypeStruct((b, s, wg.shape[1]), F32)] + t_shapes,
        compiler_params=_params(2),
        name="inproj",
    )(x, ada, nw, wq, wh, wg, wt)


def _attn_consts(s):
    nb = s // MOBA_BLOCK
    key_blk = np.arange(s) // MOBA_BLOCK
    gm = np.zeros((LANES, LANES), np.float32)
    for b in range(nb):
        for b2 in range(nb):
            gm[b, b * PAIR_SLOTS + b2] = 1.0
    er = np.zeros((s, LANES), np.float32)
    er[np.arange(s), key_blk] = 1.0
    return jnp.asarray(gm, BF16), jnp.asarray(er, BF16)


def _attn_prompt_kernel(spp, pt_ref, q_ref, kt_ref, vt_ref, to_ref, tp_ref, gm_ref, er_ref, qts_ref, ktn_ref, ts_ref, cache_ref,
                        o_ref, ps_ref, idx_ref, po_ref, kaug_ref, vaug_ref, qb_ref, sc_ref, pages_ref, sem):
    s_len = q_ref.shape[1]
    nb = s_len // MOBA_BLOCK
    blk = MOBA_BLOCK

    step = pl.program_id(0) * pl.num_programs(1) + pl.program_id(1)
    n_steps = pl.num_programs(0) * pl.num_programs(1)
    n_batches = pt_ref.shape[1] // PAGES_PER_STEP
    work = [(u, g) for u in range(spp) for g in range(n_batches)]
    n_ahead = min(PAGE_SLOTS - 1, len(work))

    def batch_copies(w, of_step):
        u, g = work[w]
        slot = w % PAGE_SLOTS
        page = lambda j: 0 if of_step is None else pt_ref[of_step * spp + u, g * PAGES_PER_STEP + j]
        return [pltpu.make_async_copy(cache_ref.at[page(j)], pages_ref.at[slot, j], sem.at[slot])
                for j in range(PAGES_PER_STEP)]

    def start_batch(w, of_step):
        for c in batch_copies(w, of_step):
            c.start()

    def finish_batch(w):
        u, g = work[w]
        n = step * spp + u
        for c in batch_copies(w, None):
            c.wait()
        if w + n_ahead < len(work):
            start_batch(w + n_ahead, step)
        if g == 0:
            _sample_query(qts_ref, n, qb_ref)
        _page_scores(pages_ref.at[w % PAGE_SLOTS], qb_ref, sc_ref, g * PAGES_PER_STEP)
        if g == n_batches - 1:
            _select_blocks(sc_ref[...], n, qts_ref, ktn_ref, ts_ref, ps_ref, idx_ref, po_ref, u)

    order = list(range(nb - 1, -1, -1))
    due = [-1] + [min(nb - 1, ((w - 1) * max(1, nb // 2)) // len(work)) for w in range(1, len(work))]

    @pl.when(step == 0)
    def _():
        for w in range(n_ahead):
            start_batch(w, step)
    k_rows = kt_ref[0].reshape(PAIR, s_len).T
    kaug_ref[:, 0:PAIR] = k_rows.astype(BF16)
    kaug_ref[:, PAIR:2 * PAIR] = er_ref[...]
    vaug_ref[0:PAIR, :] = vt_ref[0].reshape(PAIR, s_len).astype(BF16)
    vaug_ref[PAIR:, :] = jnp.where(lax.broadcasted_iota(I32, (V_PAD_ROWS, s_len), 0) == 0, 1.0, 0.0).astype(BF16)
    q_t = q_ref[0].T * (SCALE * LOG2E)
    dim_row = lax.broadcasted_iota(I32, (PAIR, blk), 0)
    slot = lax.broadcasted_iota(I32, (LANES, 2 * blk), 0)
    t_own = jnp.concatenate([to_ref[0], to_ref[1]], axis=1)
    t_prev = jnp.concatenate([tp_ref[0], tp_ref[1]], axis=1)
    for w in range(len(work)):
        if due[w] < 0:
            finish_batch(w)

    q_hi, q_lo = [], []
    for j in range(nb):
        q2 = q_t[:, j * blk:(j + 1) * blk]
        qs = jnp.concatenate([jnp.where(dim_row < ATTN_HEAD_DIM, q2, 0.0), jnp.where(dim_row >= ATTN_HEAD_DIM, q2, 0.0)], axis=1)
        q_hi.append(qs.astype(BF16))
        q_lo.append((qs - q_hi[j].astype(F32)).astype(BF16) if j > MOBA_TOPK else None)

    mask_rows = [None] * nb
    if nb > MOBA_TOPK + 1:
        slot_k = lax.broadcasted_iota(I32, (LANES, PAIR), 0)
        dmean = jnp.zeros((LANES, PAIR), F32)
        for b in range(nb - 1):
            mb = jnp.sum(k_rows[b * blk:(b + 1) * blk], axis=0, keepdims=True) * (1.0 / blk)
            dmean = dmean + jnp.where(slot_k % PAIR_SLOTS == b, mb, 0.0) - jnp.where(slot_k // PAIR_SLOTS == b, mb, 0.0)
        d_hi = dmean.astype(BF16)
        d_lo = (dmean - d_hi.astype(F32)).astype(BF16)
        d_stack = jnp.concatenate([d_hi, d_hi, d_lo], axis=1)
        blk_b = slot // PAIR_SLOTS
        blk_b2 = slot % PAIR_SLOTS
        diffs = {j: jnp.dot(d_stack, jnp.concatenate([q_hi[j], q_lo[j], q_hi[j]], axis=0), preferred_element_type=F32)
                 for j in range(MOBA_TOPK + 1, nb)}
        ranks = {}
        for j, diff in diffs.items():
            beats = ((diff > 0.0) | ((diff == 0.0) & (blk_b2 < blk_b))) & (blk_b2 < j) & (blk_b < j)
            ranks[j] = jnp.dot(gm_ref[...], jnp.where(beats, 1.0, 0.0).astype(BF16), preferred_element_type=F32)
        for j, rank in ranks.items():
            keep = (slot == j) | ((slot < j) & (rank < MOBA_TOPK)) | (slot >= PAIR_SLOTS)
            mask_rows[j] = jnp.where(keep, 0.0, NEG).astype(BF16)

    def scores(j):
        n_keys = (j + 1) * blk
        if mask_rows[j] is None:
            return jnp.dot(kaug_ref[0:n_keys, 0:PAIR], q_hi[j], preferred_element_type=F32)
        return jnp.dot(kaug_ref[0:n_keys, :], jnp.concatenate([q_hi[j], mask_rows[j]], axis=0), preferred_element_type=F32)

    sc_next = scores(order[0])
    for pos, j in enumerate(order):
        n_keys = (j + 1) * blk
        sc = sc_next
        if pos + 1 < nb:
            sc_next = scores(order[pos + 1])
        pieces = []
        if j >= 2:
            pieces.append(sc[:n_keys - 2 * blk])
        if j >= 1:
            pieces.append(sc[n_keys - 2 * blk:n_keys - blk] + t_prev)
        pieces.append(sc[n_keys - blk:] + t_own)
        sc = jnp.concatenate(pieces, axis=0) if len(pieces) > 1 else pieces[0]
        m = jnp.max(sc, axis=0, keepdims=True)
        p = jnp.exp2(sc - m).astype(BF16)
        o2 = jnp.dot(vaug_ref[:, 0:n_keys], p, preferred_element_type=F32)
        o2 = o2[0:PAIR] * (1.0 / o2[PAIR:PAIR + 1])
        heads = jnp.concatenate([o2[:ATTN_HEAD_DIM, :blk], o2[ATTN_HEAD_DIM:, blk:]], axis=0)
        o_ref[0, j * blk:(j + 1) * blk, :] = heads.T
        for w in range(len(work)):
            if due[w] == pos:
                finish_batch(w)

    @pl.when(step + 1 < n_steps)
    def _():
        for w in range(n_ahead):
            start_batch(w, step + 1)


def _attention(qa, kt, vt, t_own, t_prev, page_table, qt_s, kt_new, t_samp, cache_kt):
    b, s, _ = qa.shape
    db, n_pages = page_table.shape
    n_pairs = ATTN_HEADS // 2
    past = n_pages * PAGE_SIZE
    assert s % MOBA_BLOCK == 0 and s // MOBA_BLOCK <= PAIR_SLOTS
    assert db % (b * n_pairs) == 0 and n_pages % PAGES_PER_STEP == 0
    assert past % MOBA_BLOCK == 0 and MOBA_TOPK <= past // MOBA_BLOCK <= LANES
    spp = db // (b * n_pairs)
    gm, er = _attn_consts(s)
    wsel = MOBA_TOPK * MOBA_BLOCK
    const = lambda shape: pl.BlockSpec(shape, lambda i, p, pt: (0,) * len(shape), pipeline_mode=pl.Buffered(1))
    per_sample = lambda w: pl.BlockSpec((spp, ATTN_HEADS, w), lambda i, p, pt: (i * n_pairs + p, 0, 0))
    return pl.pallas_call(
        functools.partial(_attn_prompt_kernel, spp),
        grid_spec=pltpu.PrefetchScalarGridSpec(
            num_scalar_prefetch=1,
            grid=(b, n_pairs),
            in_specs=[pl.BlockSpec((1, s, PAIR), lambda i, p, pt: (i, 0, p)),
                      pl.BlockSpec((1, 2, ATTN_HEAD_DIM, s), lambda i, p, pt: (i, p, 0, 0)),
                      pl.BlockSpec((1, 2, ATTN_HEAD_DIM, s), lambda i, p, pt: (i, p, 0, 0)),
                      pl.BlockSpec((2, MOBA_BLOCK, MOBA_BLOCK), lambda i, p, pt: (p, 0, 0)),
                      pl.BlockSpec((2, MOBA_BLOCK, MOBA_BLOCK), lambda i, p, pt: (p, 0, 0)),
                      const(gm.shape), const(er.shape), const(qt_s.shape), const(kt_new.shape), const(t_samp.shape),
                      pl.BlockSpec(memory_space=pl.ANY)],
            out_specs=[pl.BlockSpec((1, s, PAIR), lambda i, p, pt: (i, 0, p)),
                       per_sample(wsel), per_sample(LANES), per_sample(LANES)],
            scratch_shapes=[pltpu.VMEM((s, 2 * PAIR), BF16), pltpu.VMEM((PAIR + V_PAD_ROWS, s), BF16),
                            pltpu.VMEM((ATTN_WIDTH, PAGE_SIZE), F32), pltpu.VMEM((ATTN_HEADS, past), F32),
                            pltpu.VMEM((PAGE_SLOTS, PAGES_PER_STEP, ATTN_WIDTH, PAGE_SIZE), F32),
                            pltpu.SemaphoreType.DMA((PAGE_SLOTS,))]),
        out_shape=[jax.ShapeDtypeStruct((b, s, ATTN_WIDTH), F32),
                   jax.ShapeDtypeStruct((db, ATTN_HEADS, wsel), F32),
                   jax.ShapeDtypeStruct((db, ATTN_HEADS, LANES), I32),
                   jax.ShapeDtypeStruct((db, ATTN_HEADS, LANES), F32)],
        compiler_params=_params(2),
        name="attention",
    )(page_table, qa, kt, vt, t_own, t_prev, gm, er, qt_s, kt_new, t_samp, cache_kt)


def _lower_bound(logits, axis):
    m = jnp.max(logits, axis=axis, keepdims=True)
    e = jnp.exp(logits - m)
    first = e[0:1] if axis == 0 else e[:, 0:1]
    return first / jnp.sum(e, axis=axis, keepdims=True)


def _gated_out(o, gw, og):
    var = jnp.mean(o * o, axis=-1, keepdims=True)
    return (o * lax.rsqrt(var + EPS) * gw) * (og * _sigmoid(og))


def _cumsum_rows(tri, x):
    hi = x.astype(BF16)
    rest = x - hi.astype(F32)
    mid = rest.astype(BF16)
    lo = (rest - mid.astype(F32)).astype(BF16)
    n = x.shape[1]
    y = jnp.dot(tri, jnp.concatenate([hi, mid, lo], axis=1), preferred_element_type=F32)
    return y[:, :n] + y[:, n:2 * n] + y[:, 2 * n:]


def _hgrn_prompt_kernel(spp, pp_ref, q_ref, f_ref, v_ref, og_ref, lbl_ref, gw_ref, tri_ref, ps_ref, po_ref, vn_ref, cache_ref,
                        o_ref, s_ref, oa_ref, shift_s, vrow_s, vpages_ref, vsem):
    step = pl.program_id(0) * pl.num_programs(1) + pl.program_id(1)
    n_steps = pl.num_programs(0) * pl.num_programs(1)

    @pl.when(step == 0)
    def _():
        for u in range(spp):
            for c in _value_copies(pp_ref, cache_ref, vpages_ref, vsem, u, u):
                c.start()

    for u in range(spp):
        for c in _value_copies(pp_ref, cache_ref, vpages_ref, vsem, None, u):
            c.wait()
        _weighted_values(vpages_ref, ps_ref, po_ref, vn_ref, oa_ref, u)

    t_len = q_ref.shape[1]
    n_sub = LANES // SUB
    half = SUB // 2
    dim = HGRN_DIM
    lb = _lower_bound(lbl_ref[...], 0)
    gw = gw_ref[...]
    tri = tri_ref[...]
    half_row = lax.broadcasted_iota(I32, (n_sub, half, dim), 1)
    trans_b = (((1,), (1,)), ((), ()))

    def prepare(t):
        rows = pl.ds(pl.multiple_of(t * LANES, LANES), LANES)
        qr = q_ref[0, rows, :]
        q = qr * _sigmoid(qr)
        f = lb + (1.0 - lb) * _sigmoid(f_ref[0, rows, :])
        v = v_ref[0, rows, :]
        cum = _cumsum_rows(tri, jnp.log2(f))
        return dict(rows=rows, q=q, kk=1.0 - f, v=v, vb=v.astype(BF16), cum=cum)

    def split(p):
        c3 = p["cum"].reshape(n_sub, SUB, dim)
        start = jnp.concatenate([jnp.zeros((1, 1, dim), F32), c3[:n_sub - 1, SUB - 1:SUB, :]], axis=0)
        p.update(c3=c3, start=start, b3=c3 - start, c_end=c3[n_sub - 1, SUB - 1:SUB, :],
                 q3=p["q"].reshape(n_sub, SUB, dim), k3=p["kk"].reshape(n_sub, SUB, dim))

    def first_products(p):
        p["upd"] = lax.dot_general(p["vb"], (p["kk"] * jnp.exp2(p["c_end"] - p["cum"])).astype(BF16),
                                   (((0,), (0,)), ((), ())), preferred_element_type=F32)
        qt3 = (p["q3"] * jnp.exp2(p["b3"])).astype(BF16)
        p["scores"] = []
        for i in range(1, n_sub):
            kt = (p["k3"][:i] * jnp.exp2(p["start"][i:i + 1] - p["c3"][:i])).reshape(i * SUB, dim).astype(BF16)
            p["scores"].append(lax.dot_general(qt3[i], kt, trans_b, preferred_element_type=F32).astype(BF16))

    def within_sub_chunks(p, slot):
        b3, q3 = p["b3"], p["q3"]
        shift_s[slot] = (b3 - jnp.log2(p["k3"])).reshape(LANES, dim)
        vrow_s[slot] = p["v"]
        key_row = lambda ref, s: jnp.concatenate(
            [jnp.broadcast_to(ref[slot, c * SUB + s:c * SUB + s + 1, :], (1, half, dim)) for c in range(n_sub)], axis=0)
        b_lo, b_hi, q_lo, q_hi = b3[:, :half], b3[:, half:], q3[:, :half], q3[:, half:]
        od_lo = jnp.zeros((n_sub, half, dim), F32)
        od_hi = jnp.zeros((n_sub, half, dim), F32)
        for s in range(SUB):
            bs, vs = key_row(shift_s, s), key_row(vrow_s, s)
            w_hi = jnp.sum(q_hi * jnp.exp2(b_hi - bs), axis=-1, keepdims=True)
            if s < half:
                w_lo = jnp.sum(q_lo * jnp.exp2(b_lo - bs), axis=-1, keepdims=True)
                od_lo = od_lo + jnp.where(half_row >= s, w_lo, 0.0) * vs
                od_hi = od_hi + w_hi * vs
            else:
                od_hi = od_hi + jnp.where(half_row >= s - half, w_hi, 0.0) * vs
        p["o_sub"] = jnp.concatenate([od_lo, od_hi], axis=1).reshape(LANES, dim)

    def tiles(g, st):
        group = [prepare(g * TILE_UNROLL + u) for u in range(TILE_UNROLL)]
        for p in group:
            split(p)
            first_products(p)
        for p in group:
            p["o_state"] = lax.dot_general((p["q"] * jnp.exp2(p["cum"])).astype(BF16), st.astype(BF16), trans_b,
                                           preferred_element_type=F32)
            st = st * jnp.exp2(p["c_end"]) + p["upd"]
        for u, p in enumerate(group):
            within_sub_chunks(p, u)
        for p in group:
            cross = [jnp.dot(a, p["vb"][0:(i + 1) * SUB], preferred_element_type=F32) for i, a in enumerate(p["scores"])]
            o = p["o_sub"] + p["o_state"] + jnp.concatenate([jnp.zeros((SUB, dim), F32)] + cross, axis=0)
            o_ref[0, p["rows"], :] = _gated_out(o, gw, og_ref[0, p["rows"], :])
        return st

    st = lax.fori_loop(0, t_len // (LANES * TILE_UNROLL), tiles, jnp.zeros((dim, dim), F32))
    s_ref[0, 0] = st.T

    @pl.when(step + 1 < n_steps)
    def _():
        for u in range(spp):
            for c in _value_copies(pp_ref, cache_ref, vpages_ref, vsem, (step + 1) * spp + u, u):
                c.start()


def _hgrn_tri():
    r = np.arange(LANES)
    return jnp.asarray((r[None, :] <= r[:, None]).astype(np.float32), BF16)


def _hgrn_prompt(hg, lb_logits, gw, phys, p_sel, p_own, v_new, cache_vt):
    b, s, _ = hg.shape
    db = p_sel.shape[0]
    assert s % (LANES * TILE_UNROLL) == 0 and db % (b * HGRN_HEADS) == 0
    spp = db // (b * HGRN_HEADS)
    col = lambda part: pl.BlockSpec((1, s, HGRN_DIM), lambda i, h, pp, part=part: (i, 0, part * HGRN_HEADS + h))
    const = lambda shape: pl.BlockSpec(shape, lambda i, h, pp: (0,) * len(shape), pipeline_mode=pl.Buffered(1))
    per_sample = lambda w: pl.BlockSpec((spp, ATTN_HEADS, 1, w), lambda i, h, pp: (i * HGRN_HEADS + h, 0, 0, 0))
    return pl.pallas_call(
        functools.partial(_hgrn_prompt_kernel, spp),
        grid_spec=pltpu.PrefetchScalarGridSpec(
            num_scalar_prefetch=1,
            grid=(b, HGRN_HEADS),
            in_specs=[col(0), col(1), col(2), col(3),
                      pl.BlockSpec((lb_logits.shape[0], HGRN_DIM), lambda i, h, pp: (0, h)),
                      const(gw.shape), const((LANES, LANES)),
                      per_sample(p_sel.shape[3]), per_sample(LANES), per_sample(ATTN_HEAD_DIM),
                      pl.BlockSpec(memory_space=pl.ANY)],
            out_specs=[pl.BlockSpec((1, s, HGRN_DIM), lambda i, h, pp: (i, 0, h)),
                       pl.BlockSpec((1, 1, HGRN_DIM, HGRN_DIM), lambda i, h, pp: (i, h, 0, 0)),
                       per_sample(ATTN_HEAD_DIM)],
            scratch_shapes=[pltpu.VMEM((TILE_UNROLL, LANES, HGRN_DIM), F32), pltpu.VMEM((TILE_UNROLL, LANES, HGRN_DIM), F32),
                            pltpu.VMEM((spp * N_SEL_TILES, ATTN_HEAD_DIM, PAGE_SIZE), F32),
                            pltpu.SemaphoreType.DMA((spp,))]),
        out_shape=[jax.ShapeDtypeStruct((b, s, HGRN_WIDTH), F32),
                   jax.ShapeDtypeStruct((b, HGRN_HEADS, HGRN_DIM, HGRN_DIM), F32),
                   jax.ShapeDtypeStruct((db, ATTN_HEADS, 1, ATTN_HEAD_DIM), F32)],
        compiler_params=_params(2),
        name="hgrn_prompt",
    )(phys, hg, hg, hg, hg, lb_logits, gw, _hgrn_tri(), p_sel, p_own, v_new, cache_vt)


def _hgrn_sample_kernel(xt_ref, hg_ref, lblt_ref, gw_ref, st_ref, o_ref, so_ref):
    width = HGRN_WIDTH
    xt = xt_ref[0]
    lane = lax.broadcasted_iota(I32, xt.shape, 1)
    lb = _lower_bound(lblt_ref[...], 1)
    for u in range(st_ref.shape[0]):
        n = pl.program_id(0) * st_ref.shape[0] + u
        cols = jnp.sum(jnp.where(lane == n, xt, 0.0), axis=1, keepdims=True)
        hg = hg_ref[u]
        for h in range(HGRN_HEADS):
            rows = slice(h * HGRN_DIM, (h + 1) * HGRN_DIM)
            qr = cols[rows]
            q = qr * _sigmoid(qr)
            lbh = lb[rows]
            f = lbh + (1.0 - lbh) * _sigmoid(cols[width + h * HGRN_DIM:width + (h + 1) * HGRN_DIM])
            v = hg[:, 2 * width + h * HGRN_DIM:2 * width + (h + 1) * HGRN_DIM]
            og = hg[:, 3 * width + h * HGRN_DIM:3 * width + (h + 1) * HGRN_DIM]
            s_new = f * st_ref[u, h] + (1.0 - f) * v
            so_ref[u, h] = s_new
            o = jnp.sum(q * s_new, axis=0, keepdims=True)
            o_ref[u, :, rows] = _gated_out(o, gw_ref[...], og)


def _hgrn_sample(xt, hg, lb_logits_t, gw, state):
    db = state.shape[0]
    ns = SAMPLES_PER_STEP
    assert db % ns == 0
    return pl.pallas_call(
        _hgrn_sample_kernel,
        grid=(db // ns,),
        in_specs=[_const_spec(xt.shape),
                  pl.BlockSpec((ns, 1, hg.shape[2]), lambda n: (n, 0, 0)),
                  _const_spec(lb_logits_t.shape), _const_spec(gw.shape),
                  pl.BlockSpec((ns,) + state.shape[1:], lambda n: (n, 0, 0, 0))],
        out_specs=[pl.BlockSpec((ns, 1, HGRN_WIDTH), lambda n: (n, 0, 0)),
                   pl.BlockSpec((ns,) + state.shape[1:], lambda n: (n, 0, 0, 0))],
        out_shape=[jax.ShapeDtypeStruct((db, 1, HGRN_WIDTH), F32),
                   jax.ShapeDtypeStruct(state.shape, F32)],
        compiler_params=_params(1),
        name="hgrn_sample",
    )(xt, hg, lb_logits_t, gw, state)


def _post_kernel(x_ref, oa_ref, oh_ref, gt_ref, ada_ref, wua_ref, wuh_ref, wo_ref, nf_ref, w1_ref, w2_ref, fn_ref, y_ref):
    d = x_ref.shape[-1]
    ada = ada_ref[0]
    g = gt_ref[0]
    ua = jnp.dot(oa_ref[0].astype(BF16), wua_ref[...], preferred_element_type=F32)
    uh = jnp.dot(oh_ref[0].astype(BF16), wuh_ref[...], preferred_element_type=F32)
    merged = _sigmoid(g[:, :d]) * ua + _sigmoid(g[:, d:]) * uh
    x1 = x_ref[0] + ada[:, 2 * d:3 * d] * jnp.dot(merged.astype(BF16), wo_ref[...], preferred_element_type=F32)
    h2 = _rms_mod(x1, nf_ref[...], ada[:, 3 * d:4 * d], ada[:, 4 * d:5 * d]).astype(BF16)
    a = jnp.maximum(jnp.dot(h2, w1_ref[...], preferred_element_type=F32), 0.0)
    x2 = x1 + ada[:, 5 * d:6 * d] * jnp.dot((a * a).astype(BF16), w2_ref[...], preferred_element_type=F32)
    var = jnp.mean(x2 * x2, axis=-1, keepdims=True)
    y_ref[0] = x2 * lax.rsqrt(var + EPS) * fn_ref[...]


def _post(x, oa, oh, gt, ada, wua, wuh, wo, nf, w1, w2, fn, tm):
    b, s, d = x.shape
    r = ada.shape[1]
    assert s % tm == 0 and r in (1, s) and (r == 1 or tm == s)
    row = lambda w: pl.BlockSpec((1, tm, w), lambda i, j: (i, j, 0))
    return pl.pallas_call(
        _post_kernel,
        grid=(b, s // tm),
        in_specs=[row(d), row(oa.shape[2]), row(oh.shape[2]), row(gt.shape[2]),
                  pl.BlockSpec((1, r if r == 1 else tm, N_ADA * d), lambda i, j: (i, 0, 0)),
                  _const_spec(wua.shape), _const_spec(wuh.shape), _const_spec(wo.shape), _const_spec(nf.shape),
                  _const_spec(w1.shape), _const_spec(w2.shape), _const_spec(fn.shape)],
        out_specs=row(d),
        out_shape=jax.ShapeDtypeStruct((b, s, d), F32),
        compiler_params=_params(2),
        name="post",
    )(x, oa, oh, gt, ada, wua, wuh, wo, nf, w1, w2, fn)


def _sample_query(qt_ref, n, qb_ref):
    qt = qt_ref[0]
    pick = lax.broadcasted_iota(I32, qt.shape, 1) == n
    qb_ref[...] = jnp.broadcast_to(jnp.sum(jnp.where(pick, qt, 0.0), axis=1, keepdims=True), qb_ref.shape)


def _page_scores(pages_ref, qb_ref, sc_ref, first_page):
    for h in range(ATTN_HEADS):
        rows = slice(h * ATTN_HEAD_DIM, (h + 1) * ATTN_HEAD_DIM)
        qh = qb_ref[rows, :]
        for j in range(PAGES_PER_STEP):
            lanes = slice((first_page + j) * PAGE_SIZE, (first_page + j + 1) * PAGE_SIZE)
            sc_ref[h:h + 1, lanes] = jnp.sum(pages_ref[j, rows, :] * qh, axis=0, keepdims=True)


def _select_blocks(sc, n, qt_ref, kt_ref, ts_ref, ps_ref, idx_ref, po_ref, u):
    n_blk = sc.shape[1] // MOBA_BLOCK
    lane = lax.broadcasted_iota(I32, (ATTN_HEADS, LANES), 1).astype(F32)
    bs = jnp.full((ATTN_HEADS, LANES), -jnp.inf, F32)
    for b in range(n_blk):
        tot = jnp.sum(sc[:, b * MOBA_BLOCK:(b + 1) * MOBA_BLOCK], axis=1, keepdims=True)
        bs = jnp.where(lane == float(b), tot * (1.0 / MOBA_BLOCK), bs)
    picks = []
    for _ in range(MOBA_TOPK):
        m = jnp.max(bs, axis=1, keepdims=True)
        i = jnp.min(jnp.where(bs == m, lane, float(LANES)), axis=1, keepdims=True)
        picks.append(i)
        bs = jnp.where(lane == i, -jnp.inf, bs)
    ts = ts_ref[...].reshape(ATTN_HEADS, MOBA_BLOCK + LANES)
    logits = []
    for i in picks:
        blk = jnp.zeros((ATTN_HEADS, MOBA_BLOCK), F32)
        for b in range(n_blk):
            blk = jnp.where(i == float(b), sc[:, b * MOBA_BLOCK:(b + 1) * MOBA_BLOCK], blk)
        logits.append(blk * SCALE + jnp.where(i == float(n_blk - 1), ts[:, :MOBA_BLOCK], 0.0))
    lg = jnp.concatenate(logits, axis=1)
    prod = qt_ref[0] * kt_ref[0].reshape(ATTN_WIDTH, -1)
    own_all = jnp.sum(prod.reshape(ATTN_HEADS, ATTN_HEAD_DIM, prod.shape[1]), axis=1)
    pick = lax.broadcasted_iota(I32, own_all.shape, 1) == n
    own = jnp.sum(jnp.where(pick, own_all, 0.0), axis=1, keepdims=True) * SCALE + ts[:, MOBA_BLOCK:MOBA_BLOCK + 1]
    m = jnp.maximum(jnp.max(lg, axis=1, keepdims=True), own)
    e = jnp.exp(lg - m)
    eo = jnp.exp(own - m)
    den = jnp.sum(e, axis=1, keepdims=True) + eo
    ps_ref[u] = e / den
    po_ref[u] = jnp.broadcast_to(eo / den, (ATTN_HEADS, LANES))
    idx = jnp.zeros((ATTN_HEADS, LANES), F32)
    for t, i in enumerate(picks):
        idx = jnp.where(lane == float(t), i, idx)
    idx_ref[u] = idx.astype(I32)


N_SEL_PAGES = MOBA_TOPK * (MOBA_BLOCK // PAGE_SIZE)


N_SEL_TILES = ATTN_HEADS * N_SEL_PAGES


def _value_copies(pp_ref, cache_ref, vpages_ref, sem, n, u):
    page = lambda h, j: 0 if n is None else pp_ref[(n * ATTN_HEADS + h) * N_SEL_PAGES + j]
    return [pltpu.make_async_copy(cache_ref.at[page(h, j), h],
                                  vpages_ref.at[u * N_SEL_TILES + h * N_SEL_PAGES + j], sem.at[u])
            for h in range(ATTN_HEADS) for j in range(N_SEL_PAGES)]


def _weighted_values(vpages_ref, ps_ref, po_ref, vn_ref, o_ref, u):
    acc = []
    for h in range(ATTN_HEADS):
        ps = ps_ref[u, h]
        a = jnp.zeros((ATTN_HEAD_DIM, PAGE_SIZE), F32)
        for j in range(N_SEL_PAGES):
            a = a + vpages_ref[u * N_SEL_TILES + h * N_SEL_PAGES + j] * ps[:, j * PAGE_SIZE:(j + 1) * PAGE_SIZE]
        acc.append(a)
    o = lax.dot_general(jnp.ones((SUBLANES, PAGE_SIZE), F32), jnp.concatenate(acc, axis=0), (((1,), (1,)), ((), ())),
                        preferred_element_type=F32, precision=HIGHEST)
    for h in range(ATTN_HEADS):
        o_ref[u, h] = (o[0:1, h * ATTN_HEAD_DIM:(h + 1) * ATTN_HEAD_DIM]
                       + po_ref[u, h][:, 0:ATTN_HEAD_DIM] * vn_ref[u, h])


def kernel(x_prompt, x_sample, cache_k, cache_v, state_hgrn, page_table, c_prompt, c_sample, rel_bias, hgrn_lb_logits,
           w_ada, b_ada, norm_mix_w, w_in, hgrn_gnorm_w, w_up_attn, w_up_hgrn, w_out, norm_ffn_w, w_ff1, w_ff2, final_norm_w):
    depth = w_in.shape[0]
    assert depth == 1, "one-layer trunk"
    b, s, d = x_prompt.shape
    db, ds, _ = x_sample.shape
    assert ds == 1
    past = page_table.shape[1] * PAGE_SIZE

    w = w_in[0]
    a0, a1, a2 = ATTN_WIDTH, 2 * ATTN_WIDTH, 3 * ATTN_WIDTH
    h0 = a2 + 4 * HGRN_WIDTH
    wq = w[:, :a0].astype(BF16)
    wh = w[:, a2:h0].astype(BF16)
    wg = w[:, h0:].astype(BF16)
    wt_kv = w[:, a0:a2].T.astype(BF16)
    wt_s = jnp.concatenate([w[:, a0:a2], w[:, :a0], w[:, a2:a2 + 2 * HGRN_WIDTH]], axis=1).T.astype(BF16)
    wua, wuh, wo = w_up_attn[0].astype(BF16), w_up_hgrn[0].astype(BF16), w_out[0].astype(BF16)
    w1, w2 = w_ff1[0].astype(BF16), w_ff2[0].astype(BF16)
    fn = final_norm_w.reshape(1, d)

    ada = _ada(jnp.concatenate([c_prompt, c_sample], axis=0), w_ada[0], b_ada)
    ada_p = ada[:b].reshape(b, 1, N_ADA * d)
    ada_s = ada[b:].reshape(1, db, N_ADA * d)
    t_own, t_prev, t_samp = _bias_tables(rel_bias)

    xs = x_sample.reshape(1, db, d)
    qa_s, hg_s, gt_s, kt_s, vt_s, qt_s, qht_s, fht_s = _inproj(xs, ada_s, norm_mix_w, wq, wh, wg, wt_s, db)
    qa, hg, gt, kt, vt = _inproj(x_prompt, ada_p, norm_mix_w, wq, wh, wg, wt_kv, ROW_TILE)
    cache_kt = jnp.transpose(cache_k[0], (0, 2, 3, 1))
    cache_vt = jnp.transpose(cache_v[0], (0, 2, 3, 1))

    o_attn, p_sel, idx, p_own = _attention(qa, kt, vt, t_own, t_prev, page_table, qt_s, kt_s, t_samp,
                                           cache_kt.reshape(-1, ATTN_WIDTH, PAGE_SIZE))
    ppb = MOBA_BLOCK // PAGE_SIZE
    logical = idx[:, :, :MOBA_TOPK, None] * ppb + jnp.arange(ppb, dtype=I32)
    phys = jnp.take_along_axis(page_table, logical.reshape(db, -1), axis=1).reshape(-1)
    v_new = jnp.transpose(vt_s[0], (2, 0, 1))[:, :, None, :]
    o_hgrn, state_p, o_attn_s = _hgrn_prompt(hg, hgrn_lb_logits, hgrn_gnorm_w, phys, p_sel[:, :, None, :],
                                             p_own[:, :, None, :], v_new, cache_vt)
    y_prompt = _post(x_prompt, o_attn, o_hgrn, gt, ada_p, wua, wuh, wo, norm_ffn_w, w1, w2, fn, ROW_TILE)

    o_attn_s = o_attn_s.reshape(1, db, ATTN_WIDTH)
    xt = jnp.concatenate([qht_s, fht_s], axis=1)
    o_hgrn_s, state_s = _hgrn_sample(xt, hg_s.reshape(db, 1, -1), hgrn_lb_logits.T, hgrn_gnorm_w, state_hgrn[0])
    y_sample = _post(xs, o_attn_s, o_hgrn_s.reshape(1, db, HGRN_WIDTH), gt_s, ada_s, wua, wuh, wo, norm_ffn_w, w1, w2, fn, db)

    to_rows = lambda t: jnp.transpose(t, (0, 3, 1, 2))[None]
    return (y_prompt, y_sample.reshape(db, 1, d), to_rows(kt), to_rows(vt), state_p[None],
            to_rows(kt_s).reshape(1, db, 1, ATTN_HEADS, ATTN_HEAD_DIM), to_rows(vt_s).reshape(1, db, 1, ATTN_HEADS, ATTN_HEAD_DIM),
            state_s[None])
```

```python
import functools
import math

import numpy as np
import jax
import jax.numpy as jnp
from jax import lax
from jax.experimental import pallas as pl
from jax.experimental.pallas import tpu as pltpu

F32, BF16, I32 = jnp.float32, jnp.bfloat16, jnp.int32
HIGHEST = lax.Precision.HIGHEST

ATTN_HEADS = 8
ATTN_HEAD_DIM = 64
ATTN_WIDTH = ATTN_HEADS * ATTN_HEAD_DIM
MOBA_BLOCK = 256
MOBA_TOPK = 3
PAGE_SIZE = 128
N_BUCKETS = 32
MAX_DISTANCE = 128
HGRN_HEADS = 4
HGRN_DIM = 128
HGRN_WIDTH = HGRN_HEADS * HGRN_DIM
N_ADA = 6
EPS = 1e-6
SCALE = ATTN_HEAD_DIM ** -0.5
LOG2E = math.log2(math.e)

LANES = 128
SUBLANES = 8
VMEM_BYTES_V7X = 64 * 1024 * 1024
VMEM_LIMIT = VMEM_BYTES_V7X * 7 // 8

NEG = -1e30
SUB = 16
TILE_UNROLL = 4
ROW_TILE = 256
PAGES_PER_STEP = 16
PAGE_SLOTS = 3
SAMPLES_PER_STEP = 8
PAIR = 2 * ATTN_HEAD_DIM
PAIR_SLOTS = 8
V_PAD_ROWS = 16


def _params(n_grid):
    return pltpu.CompilerParams(dimension_semantics=("arbitrary",) * n_grid, vmem_limit_bytes=VMEM_LIMIT)


def _const_spec(shape):
    nd = len(shape)
    return pl.BlockSpec(shape, lambda *_: (0,) * nd, pipeline_mode=pl.Buffered(1))


def _sigmoid(x):
    return jax.nn.sigmoid(x)


def _rms_mod(x, w, shift, scale):
    var = jnp.mean(x * x, axis=-1, keepdims=True)
    return (x * lax.rsqrt(var + EPS) * w) * (1.0 + scale) + shift


def _ada_kernel(c_ref, w_ref, b_ref, o_ref):
    c = c_ref[...]
    o_ref[...] = jnp.dot(c * _sigmoid(c), w_ref[...], preferred_element_type=F32, precision=HIGHEST) + b_ref[...]


def _ada(c_all, w_ada, b_ada):
    n, d = c_all.shape
    width = w_ada.shape[1]
    tn = 1536
    assert width % tn == 0
    return pl.pallas_call(
        _ada_kernel,
        grid=(width // tn,),
        in_specs=[pl.BlockSpec((n, d), lambda j: (0, 0)),
                  pl.BlockSpec((d, tn), lambda j: (0, j)),
                  pl.BlockSpec((1, tn), lambda j: (0, j))],
        out_specs=pl.BlockSpec((n, tn), lambda j: (0, j)),
        out_shape=jax.ShapeDtypeStruct((n, width), F32),
        compiler_params=_params(1),
        name="ada",
    )(c_all, w_ada, b_ada)


def _bucket_np(dist):
    n = np.maximum(dist, 0)
    max_exact = N_BUCKETS // 2
    nf = np.maximum(n, max_exact).astype(np.float32)
    large = max_exact + (np.log(nf / np.float32(max_exact)) / np.float32(math.log(MAX_DISTANCE / max_exact))
                         * np.float32(N_BUCKETS - max_exact)).astype(np.int32)
    large = np.minimum(large, N_BUCKETS - 1)
    return np.where(n < max_exact, n, large).astype(np.int32)


def _bucket_tables():
    i = np.arange(MOBA_BLOCK)[:, None]
    j = np.arange(MOBA_BLOCK)[None, :]
    own = np.where(j <= i, _bucket_np(i - j), -1).astype(np.int32)
    prev = _bucket_np(MOBA_BLOCK + i - j)
    assert int(_bucket_np(np.array([MOBA_BLOCK + 1]))[0]) == N_BUCKETS - 1
    samp = np.full((1, MOBA_BLOCK + LANES), N_BUCKETS - 1, np.int32)
    samp[0, :MOBA_BLOCK] = _bucket_np(MOBA_BLOCK - np.arange(MOBA_BLOCK))
    samp[0, MOBA_BLOCK] = 0
    return own, prev, samp


def _bias_kernel(rb_ref, bo_ref, bp_ref, bs_ref, to_ref, tp_ref, ts_ref):
    h = pl.program_id(0)
    far = rb_ref[N_BUCKETS - 1, h]

    def table(bucket):
        acc = jnp.where(bucket < 0, NEG, 0.0).astype(F32)
        for k in range(N_BUCKETS - 1):
            acc = jnp.where(bucket == k, rb_ref[k, h] - far, acc)
        return acc

    to_ref[0] = table(bo_ref[...]) * LOG2E
    tp_ref[0] = table(bp_ref[...]) * LOG2E
    ts_ref[0] = table(bs_ref[...])


def _bias_tables(rel_bias):
    own, prev, samp = _bucket_tables()
    nh = rel_bias.shape[1]
    b = MOBA_BLOCK
    return pl.pallas_call(
        _bias_kernel,
        grid=(nh,),
        in_specs=[pl.BlockSpec(memory_space=pltpu.SMEM),
                  pl.BlockSpec((b, b), lambda h: (0, 0)),
                  pl.BlockSpec((b, b), lambda h: (0, 0)),
                  pl.BlockSpec((1, b + LANES), lambda h: (0, 0))],
        out_specs=[pl.BlockSpec((1, b, b), lambda h: (h, 0, 0)),
                   pl.BlockSpec((1, b, b), lambda h: (h, 0, 0)),
                   pl.BlockSpec((1, 1, b + LANES), lambda h: (h, 0, 0))],
        out_shape=[jax.ShapeDtypeStruct((nh, b, b), F32),
                   jax.ShapeDtypeStruct((nh, b, b), F32),
                   jax.ShapeDtypeStruct((nh, 1, b + LANES), F32)],
        compiler_params=_params(1),
        name="bias_tables",
    )(rel_bias, jnp.asarray(own.T), jnp.asarray(prev.T), jnp.asarray(samp))


def _inproj_kernel(n_t, x_ref, ada_ref, nw_ref, wq_ref, wh_ref, wg_ref, wt_ref, qa_ref, hg_ref, gt_ref, *t_refs):
    d = x_ref.shape[-1]
    ada = ada_ref[0]
    h = _rms_mod(x_ref[0], nw_ref[...], ada[:, 0:d], ada[:, d:2 * d]).astype(BF16)
    qa_ref[0] = jnp.dot(h, wq_ref[...], preferred_element_type=F32)
    hg_ref[0] = jnp.dot(h, wh_ref[...], preferred_element_type=F32)
    gt_ref[0] = jnp.dot(h, wg_ref[...], preferred_element_type=F32)
    t = lax.dot_general(wt_ref[...], h, (((1,), (1,)), ((), ())), preferred_element_type=F32)
    for i in range(n_t):
        ref = t_refs[i]
        ref[0] = t[i * ATTN_WIDTH:(i + 1) * ATTN_WIDTH].reshape(ref.shape[1:])


def _inproj(x, ada, nw, wq, wh, wg, wt, tm):
    b, s, d = x.shape
    r = ada.shape[1]
    n_t = wt.shape[0] // ATTN_WIDTH
    assert s % tm == 0 and r in (1, s) and (r == 1 or tm == s)
    t_shapes, t_specs = [], []
    for i in range(n_t):
        if i < 2:
            t_shapes.append(jax.ShapeDtypeStruct((b, ATTN_HEADS, ATTN_HEAD_DIM, s), F32))
            t_specs.append(pl.BlockSpec((1, ATTN_HEADS, ATTN_HEAD_DIM, tm), lambda i_, j_: (i_, 0, 0, j_)))
        else:
            t_shapes.append(jax.ShapeDtypeStruct((b, ATTN_WIDTH, s), F32))
            t_specs.append(pl.BlockSpec((1, ATTN_WIDTH, tm), lambda i_, j_: (i_, 0, j_)))
    row = lambda w: pl.BlockSpec((1, tm, w), lambda i_, j_: (i_, j_, 0))
    return pl.pallas_call(
        functools.partial(_inproj_kernel, n_t),
        grid=(b, s // tm),
        in_specs=[row(d),
                  pl.BlockSpec((1, r if r == 1 else tm, N_ADA * d), lambda i_, j_: (i_, 0, 0)),
                  _const_spec(nw.shape), _const_spec(wq.shape), _const_spec(wh.shape), _const_spec(wg.shape),
                  _const_spec(wt.shape)],
        out_specs=[row(wq.shape[1]), row(wh.shape[1]), row(wg.shape[1])] + t_specs,
        out_shape=[jax.ShapeDtypeStruct((b, s, wq.shape[1]), F32),
                   jax.ShapeDtypeStruct((b, s, wh.shape[1]), F32),
                   jax.ShapeDtypeStruct((b, s, wg.shape[1]), F32)] + t_shapes,
        compiler_params=_params(2),
        name="inproj",
    )(x, ada, nw, wq, wh, wg, wt)


def _attn_consts(s):
    nb = s // MOBA_BLOCK
    key_blk = np.arange(s) // MOBA_BLOCK
    gm = np.zeros((LANES, LANES), np.float32)
    for b in range(nb):
        for b2 in range(nb):
            gm[b, b * PAIR_SLOTS + b2] = 1.0
    er = np.zeros((s, LANES), np.float32)
    er[np.arange(s), key_blk] = 1.0
    return jnp.asarray(gm, BF16), jnp.asarray(er, BF16)


def _attn_prompt_kernel(spp, pt_ref, q_ref, kt_ref, vt_ref, to_ref, tp_ref, gm_ref, er_ref, qts_ref, ktn_ref, ts_ref, cache_ref,
                        o_ref, ps_ref, idx_ref, po_ref, kaug_ref, vaug_ref, qb_ref, sc_ref, pages_ref, sem):
    s_len = q_ref.shape[1]
    nb = s_len // MOBA_BLOCK
    blk = MOBA_BLOCK

    step = pl.program_id(0) * pl.num_programs(1) + pl.program_id(1)
    n_steps = pl.num_programs(0) * pl.num_programs(1)
    n_batches = pt_ref.shape[1] // PAGES_PER_STEP
    work = [(u, g) for u in range(spp) for g in range(n_batches)]
    n_ahead = min(PAGE_SLOTS - 1, len(work))

    def batch_copies(w, of_step):
        u, g = work[w]
        slot = w % PAGE_SLOTS
        page = lambda j: 0 if of_step is None else pt_ref[of_step * spp + u, g * PAGES_PER_STEP + j]
        return [pltpu.make_async_copy(cache_ref.at[page(j)], pages_ref.at[slot, j], sem.at[slot])
                for j in range(PAGES_PER_STEP)]

    def start_batch(w, of_step):
        for c in batch_copies(w, of_step):
            c.start()

    def finish_batch(w):
        u, g = work[w]
        n = step * spp + u
        for c in batch_copies(w, None):
            c.wait()
        if w + n_ahead < len(work):
            start_batch(w + n_ahead, step)
        if g == 0:
            _sample_query(qts_ref, n, qb_ref)
        _page_scores(pages_ref.at[w % PAGE_SLOTS], qb_ref, sc_ref, g * PAGES_PER_STEP)
        if g == n_batches - 1:
            _select_blocks(sc_ref[...], n, qts_ref, ktn_ref, ts_ref, ps_ref, idx_ref, po_ref, u)

    order = list(range(nb - 1, -1, -1))
    due = [-1] + [min(nb - 1, ((w - 1) * max(1, nb // 2)) // len(work)) for w in range(1, len(work))]

    @pl.when(step == 0)
    def _():
        for w in range(n_ahead):
            start_batch(w, step)
    k_rows = kt_ref[0].reshape(PAIR, s_len).T
    kaug_ref[:, 0:PAIR] = k_rows.astype(BF16)
    kaug_ref[:, PAIR:2 * PAIR] = er_ref[...]
    vaug_ref[0:PAIR, :] = vt_ref[0].reshape(PAIR, s_len).astype(BF16)
    vaug_ref[PAIR:, :] = jnp.where(lax.broadcasted_iota(I32, (V_PAD_ROWS, s_len), 0) == 0, 1.0, 0.0).astype(BF16)
    q_t = q_ref[0].T * (SCALE * LOG2E)
    dim_row = lax.broadcasted_iota(I32, (PAIR, blk), 0)
    slot = lax.broadcasted_iota(I32, (LANES, 2 * blk), 0)
    t_own = jnp.concatenate([to_ref[0], to_ref[1]], axis=1)
    t_prev = jnp.concatenate([tp_ref[0], tp_ref[1]], axis=1)
    for w in range(len(work)):
        if due[w] < 0:
            finish_batch(w)

    q_hi, q_lo = [], []
    for j in range(nb):
        q2 = q_t[:, j * blk:(j + 1) * blk]
        qs = jnp.concatenate([jnp.where(dim_row < ATTN_HEAD_DIM, q2, 0.0), jnp.where(dim_row >= ATTN_HEAD_DIM, q2, 0.0)], axis=1)
        q_hi.append(qs.astype(BF16))
        q_lo.append((qs - q_hi[j].astype(F32)).astype(BF16) if j > MOBA_TOPK else None)

    mask_rows = [None] * nb
    if nb > MOBA_TOPK + 1:
        slot_k = lax.broadcasted_iota(I32, (LANES, PAIR), 0)
        dmean = jnp.zeros((LANES, PAIR), F32)
        for b in range(nb - 1):
            mb = jnp.sum(k_rows[b * blk:(b + 1) * blk], axis=0, keepdims=True) * (1.0 / blk)
            dmean = dmean + jnp.where(slot_k % PAIR_SLOTS == b, mb, 0.0) - jnp.where(slot_k // PAIR_SLOTS == b, mb, 0.0)
        d_hi = dmean.astype(BF16)
        d_lo = (dmean - d_hi.astype(F32)).astype(BF16)
        d_stack = jnp.concatenate([d_hi, d_hi, d_lo], axis=1)
        blk_b = slot // PAIR_SLOTS
        blk_b2 = slot % PAIR_SLOTS
        diffs = {j: jnp.dot(d_stack, jnp.concatenate([q_hi[j], q_lo[j], q_hi[j]], axis=0), preferred_element_type=F32)
                 for j in range(MOBA_TOPK + 1, nb)}
        ranks = {}
        for j, diff in diffs.items():
            beats = ((diff > 0.0) | ((diff == 0.0) & (blk_b2 < blk_b))) & (blk_b2 < j) & (blk_b < j)
            ranks[j] = jnp.dot(gm_ref[...], jnp.where(beats, 1.0, 0.0).astype(BF16), preferred_element_type=F32)
        for j, rank in ranks.items():
            keep = (slot == j) | ((slot < j) & (rank < MOBA_TOPK)) | (slot >= PAIR_SLOTS)
            mask_rows[j] = jnp.where(keep, 0.0, NEG).astype(BF16)

    def scores(j):
        n_keys = (j + 1) * blk
        if mask_rows[j] is None:
            return jnp.dot(kaug_ref[0:n_keys, 0:PAIR], q_hi[j], preferred_element_type=F32)
        return jnp.dot(kaug_ref[0:n_keys, :], jnp.concatenate([q_hi[j], mask_rows[j]], axis=0), preferred_element_type=F32)

    sc_next = scores(order[0])
    for pos, j in enumerate(order):
        n_keys = (j + 1) * blk
        sc = sc_next
        if pos + 1 < nb:
            sc_next = scores(order[pos + 1])
        pieces = []
        if j >= 2:
            pieces.append(sc[:n_keys - 2 * blk])
        if j >= 1:
            pieces.append(sc[n_keys - 2 * blk:n_keys - blk] + t_prev)
        pieces.append(sc[n_keys - blk:] + t_own)
        sc = jnp.concatenate(pieces, axis=0) if len(pieces) > 1 else pieces[0]
        m = jnp.max(sc, axis=0, keepdims=True)
        p = jnp.exp2(sc - m).astype(BF16)
        o2 = jnp.dot(vaug_ref[:, 0:n_keys], p, preferred_element_type=F32)
        o2 = o2[0:PAIR] * (1.0 / o2[PAIR:PAIR + 1])
        heads = jnp.concatenate([o2[:ATTN_HEAD_DIM, :blk], o2[ATTN_HEAD_DIM:, blk:]], axis=0)
        o_ref[0, j * blk:(j + 1) * blk, :] = heads.T
        for w in range(len(work)):
            if due[w] == pos:
                finish_batch(w)

    @pl.when(step + 1 < n_steps)
    def _():
        for w in range(n_ahead):
            start_batch(w, step + 1)


def _attention(qa, kt, vt, t_own, t_prev, page_table, qt_s, kt_new, t_samp, cache_kt):
    b, s, _ = qa.shape
    db, n_pages = page_table.shape
    n_pairs = ATTN_HEADS // 2
    past = n_pages * PAGE_SIZE
    assert s % MOBA_BLOCK == 0 and s // MOBA_BLOCK <= PAIR_SLOTS
    assert db % (b * n_pairs) == 0 and n_pages % PAGES_PER_STEP == 0
    assert past % MOBA_BLOCK == 0 and MOBA_TOPK <= past // MOBA_BLOCK <= LANES
    spp = db // (b * n_pairs)
    gm, er = _attn_consts(s)
    wsel = MOBA_TOPK * MOBA_BLOCK
    const = lambda shape: pl.BlockSpec(shape, lambda i, p, pt: (0,) * len(shape), pipeline_mode=pl.Buffered(1))
    per_sample = lambda w: pl.BlockSpec((spp, ATTN_HEADS, w), lambda i, p, pt: (i * n_pairs + p, 0, 0))
    return pl.pallas_call(
        functools.partial(_attn_prompt_kernel, spp),
        grid_spec=pltpu.PrefetchScalarGridSpec(
            num_scalar_prefetch=1,
            grid=(b, n_pairs),
            in_specs=[pl.BlockSpec((1, s, PAIR), lambda i, p, pt: (i, 0, p)),
                      pl.BlockSpec((1, 2, ATTN_HEAD_DIM, s), lambda i, p, pt: (i, p, 0, 0)),
                      pl.BlockSpec((1, 2, ATTN_HEAD_DIM, s), lambda i, p, pt: (i, p, 0, 0)),
                      pl.BlockSpec((2, MOBA_BLOCK, MOBA_BLOCK), lambda i, p, pt: (p, 0, 0)),
                      pl.BlockSpec((2, MOBA_BLOCK, MOBA_BLOCK), lambda i, p, pt: (p, 0, 0)),
                      const(gm.shape), const(er.shape), const(qt_s.shape), const(kt_new.shape), const(t_samp.shape),
                      pl.BlockSpec(memory_space=pl.ANY)],
            out_specs=[pl.BlockSpec((1, s, PAIR), lambda i, p, pt: (i, 0, p)),
                       per_sample(wsel), per_sample(LANES), per_sample(LANES)],
            scratch_shapes=[pltpu.VMEM((s, 2 * PAIR), BF16), pltpu.VMEM((PAIR + V_PAD_ROWS, s), BF16),
                            pltpu.VMEM((ATTN_WIDTH, PAGE_SIZE), F32), pltpu.VMEM((ATTN_HEADS, past), F32),
                            pltpu.VMEM((PAGE_SLOTS, PAGES_PER_STEP, ATTN_WIDTH, PAGE_SIZE), F32),
                            pltpu.SemaphoreType.DMA((PAGE_SLOTS,))]),
        out_shape=[jax.ShapeDtypeStruct((b, s, ATTN_WIDTH), F32),
                   jax.ShapeDtypeStruct((db, ATTN_HEADS, wsel), F32),
                   jax.ShapeDtypeStruct((db, ATTN_HEADS, LANES), I32),
                   jax.ShapeDtypeStruct((db, ATTN_HEADS, LANES), F32)],
        compiler_params=_params(2),
        name="attention",
    )(page_table, qa, kt, vt, t_own, t_prev, gm, er, qt_s, kt_new, t_samp, cache_kt)


def _lower_bound(logits, axis):
    m = jnp.max(logits, axis=axis, keepdims=True)
    e = jnp.exp(logits - m)
    first = e[0:1] if axis == 0 else e[:, 0:1]
    return first / jnp.sum(e, axis=axis, keepdims=True)


def _gated_out(o, gw, og):
    var = jnp.mean(o * o, axis=-1, keepdims=True)
    return (o * lax.rsqrt(var + EPS) * gw) * (og * _sigmoid(og))


def _cumsum_rows(tri, x):
    hi = x.astype(BF16)
    rest = x - hi.astype(F32)
    mid = rest.astype(BF16)
    lo = (rest - mid.astype(F32)).astype(BF16)
    n = x.shape[1]
    y = jnp.dot(tri, jnp.concatenate([hi, mid, lo], axis=1), preferred_element_type=F32)
    return y[:, :n] + y[:, n:2 * n] + y[:, 2 * n:]


def _hgrn_prompt_kernel(spp, pp_ref, q_ref, f_ref, v_ref, og_ref, lbl_ref, gw_ref, tri_ref, ps_ref, po_ref, vn_ref, cache_ref,
                        o_ref, s_ref, oa_ref, shift_s, vrow_s, vpages_ref, vsem):
    step = pl.program_id(0) * pl.num_programs(1) + pl.program_id(1)
    n_steps = pl.num_programs(0) * pl.num_programs(1)

    par = lax.rem(step, 2)

    @pl.when(step == 0)
    def _():
        for u in range(spp):
            for c in _value_copies(pp_ref, cache_ref, vpages_ref.at[0], vsem.at[0], u, u):
                c.start()

    @pl.when(step + 1 < n_steps)
    def _():
        for u in range(spp):
            for c in _value_copies(pp_ref, cache_ref, vpages_ref.at[1 - par], vsem.at[1 - par], (step + 1) * spp + u, u):
                c.start()

    for u in range(spp):
        for c in _value_copies(pp_ref, cache_ref, vpages_ref.at[par], vsem.at[par], None, u):
            c.wait()
        _weighted_values(vpages_ref.at[par], ps_ref, po_ref, vn_ref, oa_ref, u)

    t_len = q_ref.shape[1]
    n_sub = LANES // SUB
    half = SUB // 2
    dim = HGRN_DIM
    lb = _lower_bound(lbl_ref[...], 0)
    gw = gw_ref[...]
    tri = tri_ref[...]
    half_row = lax.broadcasted_iota(I32, (n_sub, half, dim), 1)
    trans_b = (((1,), (1,)), ((), ()))

    def prepare(t):
        rows = pl.ds(pl.multiple_of(t * LANES, LANES), LANES)
        qr = q_ref[0, rows, :]
        q = qr * _sigmoid(qr)
        f = lb + (1.0 - lb) * _sigmoid(f_ref[0, rows, :])
        v = v_ref[0, rows, :]
        cum = _cumsum_rows(tri, jnp.log2(f))
        return dict(rows=rows, q=q, kk=1.0 - f, v=v, vb=v.astype(BF16), cum=cum)

    def split(p):
        c3 = p["cum"].reshape(n_sub, SUB, dim)
        start = jnp.concatenate([jnp.zeros((1, 1, dim), F32), c3[:n_sub - 1, SUB - 1:SUB, :]], axis=0)
        p.update(c3=c3, start=start, b3=c3 - start, c_end=c3[n_sub - 1, SUB - 1:SUB, :],
                 q3=p["q"].reshape(n_sub, SUB, dim), k3=p["kk"].reshape(n_sub, SUB, dim))

    def first_products(p):
        p["upd"] = lax.dot_general(p["vb"], (p["kk"] * jnp.exp2(p["c_end"] - p["cum"])).astype(BF16),
                                   (((0,), (0,)), ((), ())), preferred_element_type=F32)
        qt3 = (p["q3"] * jnp.exp2(p["b3"])).astype(BF16)
        p["scores"] = []
        for i in range(1, n_sub):
            kt = (p["k3"][:i] * jnp.exp2(p["start"][i:i + 1] - p["c3"][:i])).reshape(i * SUB, dim).astype(BF16)
            p["scores"].append(lax.dot_general(qt3[i], kt, trans_b, preferred_element_type=F32).astype(BF16))

    def within_sub_chunks(p, slot):
        b3, q3 = p["b3"], p["q3"]
        shift_s[slot] = (b3 - jnp.log2(p["k3"])).reshape(LANES, dim)
        vrow_s[slot] = p["v"]
        key_row = lambda ref, s: jnp.concatenate(
            [jnp.broadcast_to(ref[slot, c * SUB + s:c * SUB + s + 1, :], (1, half, dim)) for c in range(n_sub)], axis=0)
        b_lo, b_hi, q_lo, q_hi = b3[:, :half], b3[:, half:], q3[:, :half], q3[:, half:]
        od_lo = jnp.zeros((n_sub, half, dim), F32)
        od_hi = jnp.zeros((n_sub, half, dim), F32)
        for s in range(SUB):
            bs, vs = key_row(shift_s, s), key_row(vrow_s, s)
            w_hi = jnp.sum(q_hi * jnp.exp2(b_hi - bs), axis=-1, keepdims=True)
            if s < half:
                w_lo = jnp.sum(q_lo * jnp.exp2(b_lo - bs), axis=-1, keepdims=True)
                od_lo = od_lo + jnp.where(half_row >= s, w_lo, 0.0) * vs
                od_hi = od_hi + w_hi * vs
            else:
                od_hi = od_hi + jnp.where(half_row >= s - half, w_hi, 0.0) * vs
        p["o_sub"] = jnp.concatenate([od_lo, od_hi], axis=1).reshape(LANES, dim)

    def tiles(g, st):
        group = [prepare(g * TILE_UNROLL + u) for u in range(TILE_UNROLL)]
        for p in group:
            split(p)
            first_products(p)
        for p in group:
            p["o_state"] = lax.dot_general((p["q"] * jnp.exp2(p["cum"])).astype(BF16), st.astype(BF16), trans_b,
                                           preferred_element_type=F32)
            st = st * jnp.exp2(p["c_end"]) + p["upd"]
        for u, p in enumerate(group):
            within_sub_chunks(p, u)
        for p in group:
            cross = [jnp.dot(a, p["vb"][0:(i + 1) * SUB], preferred_element_type=F32) for i, a in enumerate(p["scores"])]
            o = p["o_sub"] + p["o_state"] + jnp.concatenate([jnp.zeros((SUB, dim), F32)] + cross, axis=0)
            o_ref[0, p["rows"], :] = _gated_out(o, gw, og_ref[0, p["rows"], :])
        return st

    st = lax.fori_loop(0, t_len // (LANES * TILE_UNROLL), tiles, jnp.zeros((dim, dim), F32))
    s_ref[0, 0] = st.T


def _hgrn_tri():
    r = np.arange(LANES)
    return jnp.asarray((r[None, :] <= r[:, None]).astype(np.float32), BF16)


def _hgrn_prompt(hg, lb_logits, gw, phys, p_sel, p_own, v_new, cache_vt):
    b, s, _ = hg.shape
    db = p_sel.shape[0]
    assert s % (LANES * TILE_UNROLL) == 0 and db % (b * HGRN_HEADS) == 0
    spp = db // (b * HGRN_HEADS)
    col = lambda part: pl.BlockSpec((1, s, HGRN_DIM), lambda i, h, pp, part=part: (i, 0, part * HGRN_HEADS + h))
    const = lambda shape: pl.BlockSpec(shape, lambda i, h, pp: (0,) * len(shape), pipeline_mode=pl.Buffered(1))
    per_sample = lambda w: pl.BlockSpec((spp, ATTN_HEADS, 1, w), lambda i, h, pp: (i * HGRN_HEADS + h, 0, 0, 0))
    return pl.pallas_call(
        functools.partial(_hgrn_prompt_kernel, spp),
        grid_spec=pltpu.PrefetchScalarGridSpec(
            num_scalar_prefetch=1,
            grid=(b, HGRN_HEADS),
            in_specs=[col(0), col(1), col(2), col(3),
                      pl.BlockSpec((lb_logits.shape[0], HGRN_DIM), lambda i, h, pp: (0, h)),
                      const(gw.shape), const((LANES, LANES)),
                      per_sample(p_sel.shape[3]), per_sample(LANES), per_sample(ATTN_HEAD_DIM),
                      pl.BlockSpec(memory_space=pl.ANY)],
            out_specs=[pl.BlockSpec((1, s, HGRN_DIM), lambda i, h, pp: (i, 0, h)),
                       pl.BlockSpec((1, 1, HGRN_DIM, HGRN_DIM), lambda i, h, pp: (i, h, 0, 0)),
                       per_sample(ATTN_HEAD_DIM)],
            scratch_shapes=[pltpu.VMEM((TILE_UNROLL, LANES, HGRN_DIM), F32), pltpu.VMEM((TILE_UNROLL, LANES, HGRN_DIM), F32),
                            pltpu.VMEM((2, spp * N_SEL_TILES, ATTN_HEAD_DIM, PAGE_SIZE), F32),
                            pltpu.SemaphoreType.DMA((2, spp))]),
        out_shape=[jax.ShapeDtypeStruct((b, s, HGRN_WIDTH), F32),
                   jax.ShapeDtypeStruct((b, HGRN_HEADS, HGRN_DIM, HGRN_DIM), F32),
                   jax.ShapeDtypeStruct((db, ATTN_HEADS, 1, ATTN_HEAD_DIM), F32)],
        compiler_params=_params(2),
        name="hgrn_prompt",
    )(phys, hg, hg, hg, hg, lb_logits, gw, _hgrn_tri(), p_sel, p_own, v_new, cache_vt)


def _hgrn_sample_kernel(xt_ref, hg_ref, lblt_ref, gw_ref, st_ref, o_ref, so_ref):
    width = HGRN_WIDTH
    xt = xt_ref[0]
    lane = lax.broadcasted_iota(I32, xt.shape, 1)
    lb = _lower_bound(lblt_ref[...], 1)
    for u in range(st_ref.shape[0]):
        n = pl.program_id(0) * st_ref.shape[0] + u
        cols = jnp.sum(jnp.where(lane == n, xt, 0.0), axis=1, keepdims=True)
        hg = hg_ref[u]
        for h in range(HGRN_HEADS):
            rows = slice(h * HGRN_DIM, (h + 1) * HGRN_DIM)
            qr = cols[rows]
            q = qr * _sigmoid(qr)
            lbh = lb[rows]
            f = lbh + (1.0 - lbh) * _sigmoid(cols[width + h * HGRN_DIM:width + (h + 1) * HGRN_DIM])
            v = hg[:, 2 * width + h * HGRN_DIM:2 * width + (h + 1) * HGRN_DIM]
            og = hg[:, 3 * width + h * HGRN_DIM:3 * width + (h + 1) * HGRN_DIM]
            s_new = f * st_ref[u, h] + (1.0 - f) * v
            so_ref[u, h] = s_new
            o = jnp.sum(q * s_new, axis=0, keepdims=True)
            o_ref[u, :, rows] = _gated_out(o, gw_ref[...], og)


def _hgrn_sample(xt, hg, lb_logits_t, gw, state):
    db = state.shape[0]
    ns = SAMPLES_PER_STEP
    assert db % ns == 0
    return pl.pallas_call(
        _hgrn_sample_kernel,
        grid=(db // ns,),
        in_specs=[_const_spec(xt.shape),
                  pl.BlockSpec((ns, 1, hg.shape[2]), lambda n: (n, 0, 0)),
                  _const_spec(lb_logits_t.shape), _const_spec(gw.shape),
                  pl.BlockSpec((ns,) + state.shape[1:], lambda n: (n, 0, 0, 0))],
        out_specs=[pl.BlockSpec((ns, 1, HGRN_WIDTH), lambda n: (n, 0, 0)),
                   pl.BlockSpec((ns,) + state.shape[1:], lambda n: (n, 0, 0, 0))],
        out_shape=[jax.ShapeDtypeStruct((db, 1, HGRN_WIDTH), F32),
                   jax.ShapeDtypeStruct(state.shape, F32)],
        compiler_params=_params(1),
        name="hgrn_sample",
    )(xt, hg, lb_logits_t, gw, state)


def _post_kernel(x_ref, oa_ref, oh_ref, gt_ref, ada_ref, wua_ref, wuh_ref, wo_ref, nf_ref, w1_ref, w2_ref, fn_ref, y_ref):
    d = x_ref.shape[-1]
    ada = ada_ref[0]
    g = gt_ref[0]
    ua = jnp.dot(oa_ref[0].astype(BF16), wua_ref[...], preferred_element_type=F32)
    uh = jnp.dot(oh_ref[0].astype(BF16), wuh_ref[...], preferred_element_type=F32)
    merged = _sigmoid(g[:, :d]) * ua + _sigmoid(g[:, d:]) * uh
    x1 = x_ref[0] + ada[:, 2 * d:3 * d] * jnp.dot(merged.astype(BF16), wo_ref[...], preferred_element_type=F32)
    h2 = _rms_mod(x1, nf_ref[...], ada[:, 3 * d:4 * d], ada[:, 4 * d:5 * d]).astype(BF16)
    a = jnp.maximum(jnp.dot(h2, w1_ref[...], preferred_element_type=F32), 0.0)
    x2 = x1 + ada[:, 5 * d:6 * d] * jnp.dot((a * a).astype(BF16), w2_ref[...], preferred_element_type=F32)
    var = jnp.mean(x2 * x2, axis=-1, keepdims=True)
    y_ref[0] = x2 * lax.rsqrt(var + EPS) * fn_ref[...]


def _post(x, oa, oh, gt, ada, wua, wuh, wo, nf, w1, w2, fn, tm):
    b, s, d = x.shape
    r = ada.shape[1]
    assert s % tm == 0 and r in (1, s) and (r == 1 or tm == s)
    row = lambda w: pl.BlockSpec((1, tm, w), lambda i, j: (i, j, 0))
    return pl.pallas_call(
        _post_kernel,
        grid=(b, s // tm),
        in_specs=[row(d), row(oa.shape[2]), row(oh.shape[2]), row(gt.shape[2]),
                  pl.BlockSpec((1, r if r == 1 else tm, N_ADA * d), lambda i, j: (i, 0, 0)),
                  _const_spec(wua.shape), _const_spec(wuh.shape), _const_spec(wo.shape), _const_spec(nf.shape),
                  _const_spec(w1.shape), _const_spec(w2.shape), _const_spec(fn.shape)],
        out_specs=row(d),
        out_shape=jax.ShapeDtypeStruct((b, s, d), F32),
        compiler_params=_params(2),
        name="post",
    )(x, oa, oh, gt, ada, wua, wuh, wo, nf, w1, w2, fn)


def _sample_query(qt_ref, n, qb_ref):
    qt = qt_ref[0]
    pick = lax.broadcasted_iota(I32, qt.shape, 1) == n
    qb_ref[...] = jnp.broadcast_to(jnp.sum(jnp.where(pick, qt, 0.0), axis=1, keepdims=True), qb_ref.shape)


def _page_scores(pages_ref, qb_ref, sc_ref, first_page):
    for h in range(ATTN_HEADS):
        rows = slice(h * ATTN_HEAD_DIM, (h + 1) * ATTN_HEAD_DIM)
        qh = qb_ref[rows, :]
        for j in range(PAGES_PER_STEP):
            lanes = slice((first_page + j) * PAGE_SIZE, (first_page + j + 1) * PAGE_SIZE)
            sc_ref[h:h + 1, lanes] = jnp.sum(pages_ref[j, rows, :] * qh, axis=0, keepdims=True)


def _select_blocks(sc, n, qt_ref, kt_ref, ts_ref, ps_ref, idx_ref, po_ref, u):
    n_blk = sc.shape[1] // MOBA_BLOCK
    lane = lax.broadcasted_iota(I32, (ATTN_HEADS, LANES), 1).astype(F32)
    bs = jnp.full((ATTN_HEADS, LANES), -jnp.inf, F32)
    for b in range(n_blk):
        tot = jnp.sum(sc[:, b * MOBA_BLOCK:(b + 1) * MOBA_BLOCK], axis=1, keepdims=True)
        bs = jnp.where(lane == float(b), tot * (1.0 / MOBA_BLOCK), bs)
    picks = []
    for _ in range(MOBA_TOPK):
        m = jnp.max(bs, axis=1, keepdims=True)
        i = jnp.min(jnp.where(bs == m, lane, float(LANES)), axis=1, keepdims=True)
        picks.append(i)
        bs = jnp.where(lane == i, -jnp.inf, bs)
    ts = ts_ref[...].reshape(ATTN_HEADS, MOBA_BLOCK + LANES)
    logits = []
    for i in picks:
        blk = jnp.zeros((ATTN_HEADS, MOBA_BLOCK), F32)
        for b in range(n_blk):
            blk = jnp.where(i == float(b), sc[:, b * MOBA_BLOCK:(b + 1) * MOBA_BLOCK], blk)
        logits.append(blk * SCALE + jnp.where(i == float(n_blk - 1), ts[:, :MOBA_BLOCK], 0.0))
    lg = jnp.concatenate(logits, axis=1)
    prod = qt_ref[0] * kt_ref[0].reshape(ATTN_WIDTH, -1)
    own_all = jnp.sum(prod.reshape(ATTN_HEADS, ATTN_HEAD_DIM, prod.shape[1]), axis=1)
    pick = lax.broadcasted_iota(I32, own_all.shape, 1) == n
    own = jnp.sum(jnp.where(pick, own_all, 0.0), axis=1, keepdims=True) * SCALE + ts[:, MOBA_BLOCK:MOBA_BLOCK + 1]
    m = jnp.maximum(jnp.max(lg, axis=1, keepdims=True), own)
    e = jnp.exp(lg - m)
    eo = jnp.exp(own - m)
    den = jnp.sum(e, axis=1, keepdims=True) + eo
    ps_ref[u] = e / den
    po_ref[u] = jnp.broadcast_to(eo / den, (ATTN_HEADS, LANES))
    idx = jnp.zeros((ATTN_HEADS, LANES), F32)
    for t, i in enumerate(picks):
        idx = jnp.where(lane == float(t), i, idx)
    idx_ref[u] = idx.astype(I32)


N_SEL_PAGES = MOBA_TOPK * (MOBA_BLOCK // PAGE_SIZE)


N_SEL_TILES = ATTN_HEADS * N_SEL_PAGES


def _value_copies(pp_ref, cache_ref, vpages_ref, sem, n, u):
    page = lambda h, j: 0 if n is None else pp_ref[(n * ATTN_HEADS + h) * N_SEL_PAGES + j]
    return [pltpu.make_async_copy(cache_ref.at[page(h, j), h],
                                  vpages_ref.at[u * N_SEL_TILES + h * N_SEL_PAGES + j], sem.at[u])
            for h in range(ATTN_HEADS) for j in range(N_SEL_PAGES)]


def _weighted_values(vpages_ref, ps_ref, po_ref, vn_ref, o_ref, u):
    acc = []
    for h in range(ATTN_HEADS):
        ps = ps_ref[u, h]
        a = jnp.zeros((ATTN_HEAD_DIM, PAGE_SIZE), F32)
        for j in range(N_SEL_PAGES):
            a = a + vpages_ref[u * N_SEL_TILES + h * N_SEL_PAGES + j] * ps[:, j * PAGE_SIZE:(j + 1) * PAGE_SIZE]
        acc.append(a)
    o = lax.dot_general(jnp.ones((SUBLANES, PAGE_SIZE), F32), jnp.concatenate(acc, axis=0), (((1,), (1,)), ((), ())),
                        preferred_element_type=F32, precision=HIGHEST)
    for h in range(ATTN_HEADS):
        o_ref[u, h] = (o[0:1, h * ATTN_HEAD_DIM:(h + 1) * ATTN_HEAD_DIM]
                       + po_ref[u, h][:, 0:ATTN_HEAD_DIM] * vn_ref[u, h])


def kernel(x_prompt, x_sample, cache_k, cache_v, state_hgrn, page_table, c_prompt, c_sample, rel_bias, hgrn_lb_logits,
           w_ada, b_ada, norm_mix_w, w_in, hgrn_gnorm_w, w_up_attn, w_up_hgrn, w_out, norm_ffn_w, w_ff1, w_ff2, final_norm_w):
    depth = w_in.shape[0]
    assert depth == 1, "one-layer trunk"
    b, s, d = x_prompt.shape
    db, ds, _ = x_sample.shape
    assert ds == 1
    past = page_table.shape[1] * PAGE_SIZE

    w = w_in[0]
    a0, a1, a2 = ATTN_WIDTH, 2 * ATTN_WIDTH, 3 * ATTN_WIDTH
    h0 = a2 + 4 * HGRN_WIDTH
    wq = w[:, :a0].astype(BF16)
    wh = w[:, a2:h0].astype(BF16)
    wg = w[:, h0:].astype(BF16)
    wt_kv = w[:, a0:a2].T.astype(BF16)
    wt_s = jnp.concatenate([w[:, a0:a2], w[:, :a0], w[:, a2:a2 + 2 * HGRN_WIDTH]], axis=1).T.astype(BF16)
    wua, wuh, wo = w_up_attn[0].astype(BF16), w_up_hgrn[0].astype(BF16), w_out[0].astype(BF16)
    w1, w2 = w_ff1[0].astype(BF16), w_ff2[0].astype(BF16)
    fn = final_norm_w.reshape(1, d)

    ada = _ada(jnp.concatenate([c_prompt, c_sample], axis=0), w_ada[0], b_ada)
    ada_p = ada[:b].reshape(b, 1, N_ADA * d)
    ada_s = ada[b:].reshape(1, db, N_ADA * d)
    t_own, t_prev, t_samp = _bias_tables(rel_bias)

    xs = x_sample.reshape(1, db, d)
    qa_s, hg_s, gt_s, kt_s, vt_s, qt_s, qht_s, fht_s = _inproj(xs, ada_s, norm_mix_w, wq, wh, wg, wt_s, db)
    qa, hg, gt, kt, vt = _inproj(x_prompt, ada_p, norm_mix_w, wq, wh, wg, wt_kv, ROW_TILE)
    cache_kt = jnp.transpose(cache_k[0], (0, 2, 3, 1))
    cache_vt = jnp.transpose(cache_v[0], (0, 2, 3, 1))

    o_attn, p_sel, idx, p_own = _attention(qa, kt, vt, t_own, t_prev, page_table, qt_s, kt_s, t_samp,
                                           cache_kt.reshape(-1, ATTN_WIDTH, PAGE_SIZE))
    ppb = MOBA_BLOCK // PAGE_SIZE
    logical = idx[:, :, :MOBA_TOPK, None] * ppb + jnp.arange(ppb, dtype=I32)
    phys = jnp.take_along_axis(page_table, logical.reshape(db, -1), axis=1).reshape(-1)
    v_new = jnp.transpose(vt_s[0], (2, 0, 1))[:, :, None, :]
    o_hgrn, state_p, o_attn_s = _hgrn_prompt(hg, hgrn_lb_logits, hgrn_gnorm_w, phys, p_sel[:, :, None, :],
                                             p_own[:, :, None, :], v_new, cache_vt)
    y_prompt = _post(x_prompt, o_attn, o_hgrn, gt, ada_p, wua, wuh, wo, norm_ffn_w, w1, w2, fn, ROW_TILE)

    o_attn_s = o_attn_s.reshape(1, db, ATTN_WIDTH)
    xt = jnp.concatenate([qht_s, fht_s], axis=1)
    o_hgrn_s, state_s = _hgrn_sample(xt, hg_s.reshape(db, 1, -1), hgrn_lb_logits.T, hgrn_gnorm_w, state_hgrn[0])
    y_sample = _post(xs, o_attn_s, o_hgrn_s.reshape(1, db, HGRN_WIDTH), gt_s, ada_s, wua, wuh, wo, norm_ffn_w, w1, w2, fn, db)

    to_rows = lambda t: jnp.transpose(t, (0, 3, 1, 2))[None]
    return (y_prompt, y_sample.reshape(db, 1, d), to_rows(kt), to_rows(vt), state_p[None],
            to_rows(kt_s).reshape(1, db, 1, ATTN_HEADS, ATTN_HEAD_DIM), to_rows(vt_s).reshape(1, db, 1, ATTN_HEADS, ATTN_HEAD_DIM),
            state_s[None])
```

```python
import functools
import math

import numpy as np
import jax
import jax.numpy as jnp
from jax import lax
from jax.experimental import pallas as pl
from jax.experimental.pallas import tpu as pltpu

F32, BF16, I32 = jnp.float32, jnp.bfloat16, jnp.int32
HIGHEST = lax.Precision.HIGHEST

ATTN_HEADS = 8
ATTN_HEAD_DIM = 64
ATTN_WIDTH = ATTN_HEADS * ATTN_HEAD_DIM
MOBA_BLOCK = 256
MOBA_TOPK = 3
PAGE_SIZE = 128
N_BUCKETS = 32
MAX_DISTANCE = 128
HGRN_HEADS = 4
HGRN_DIM = 128
HGRN_WIDTH = HGRN_HEADS * HGRN_DIM
N_ADA = 6
EPS = 1e-6
SCALE = ATTN_HEAD_DIM ** -0.5
LOG2E = math.log2(math.e)

LANES = 128
SUBLANES = 8
VMEM_BYTES_V7X = 64 * 1024 * 1024
VMEM_LIMIT = VMEM_BYTES_V7X * 7 // 8

NEG = -1e30
SUB = 16
TILE_UNROLL = 4
ROW_TILE = 256
PAGES_PER_STEP = 16
PAGE_SLOTS = 3
SAMPLES_PER_STEP = 8
PAIR = 2 * ATTN_HEAD_DIM
PAIR_SLOTS = 8
V_PAD_ROWS = 16


def _params(n_grid):
    return pltpu.CompilerParams(dimension_semantics=("arbitrary",) * n_grid, vmem_limit_bytes=VMEM_LIMIT)


def _const_spec(shape):
    nd = len(shape)
    return pl.BlockSpec(shape, lambda *_: (0,) * nd, pipeline_mode=pl.Buffered(1))


def _sigmoid(x):
    return jax.nn.sigmoid(x)


def _rms_mod(x, w, shift, scale):
    var = jnp.mean(x * x, axis=-1, keepdims=True)
    return (x * lax.rsqrt(var + EPS) * w) * (1.0 + scale) + shift


def _ada_kernel(c_ref, w_ref, b_ref, o_ref):
    c = c_ref[...]
    o_ref[...] = jnp.dot(c * _sigmoid(c), w_ref[...], preferred_element_type=F32, precision=HIGHEST) + b_ref[...]


def _ada(c_all, w_ada, b_ada):
    n, d = c_all.shape
    width = w_ada.shape[1]
    tn = 1536
    assert width % tn == 0
    return pl.pallas_call(
        _ada_kernel,
        grid=(width // tn,),
        in_specs=[pl.BlockSpec((n, d), lambda j: (0, 0)),
                  pl.BlockSpec((d, tn), lambda j: (0, j)),
                  pl.BlockSpec((1, tn), lambda j: (0, j))],
        out_specs=pl.BlockSpec((n, tn), lambda j: (0, j)),
        out_shape=jax.ShapeDtypeStruct((n, width), F32),
        compiler_params=_params(1),
        name="ada",
    )(c_all, w_ada, b_ada)


def _bucket_np(dist):
    n = np.maximum(dist, 0)
    max_exact = N_BUCKETS // 2
    nf = np.maximum(n, max_exact).astype(np.float32)
    large = max_exact + (np.log(nf / np.float32(max_exact)) / np.float32(math.log(MAX_DISTANCE / max_exact))
                         * np.float32(N_BUCKETS - max_exact)).astype(np.int32)
    large = np.minimum(large, N_BUCKETS - 1)
    return np.where(n < max_exact, n, large).astype(np.int32)


def _bucket_tables():
    i = np.arange(MOBA_BLOCK)[:, None]
    j = np.arange(MOBA_BLOCK)[None, :]
    own = np.where(j <= i, _bucket_np(i - j), -1).astype(np.int32)
    prev = _bucket_np(MOBA_BLOCK + i - j)
    assert int(_bucket_np(np.array([MOBA_BLOCK + 1]))[0]) == N_BUCKETS - 1
    samp = np.full((1, MOBA_BLOCK + LANES), N_BUCKETS - 1, np.int32)
    samp[0, :MOBA_BLOCK] = _bucket_np(MOBA_BLOCK - np.arange(MOBA_BLOCK))
    samp[0, MOBA_BLOCK] = 0
    return own, prev, samp


def _bias_kernel(rb_ref, bo_ref, bp_ref, bs_ref, to_ref, tp_ref, ts_ref):
    h = pl.program_id(0)
    far = rb_ref[N_BUCKETS - 1, h]

    def table(bucket):
        acc = jnp.where(bucket < 0, NEG, 0.0).astype(F32)
        for k in range(N_BUCKETS - 1):
            acc = jnp.where(bucket == k, rb_ref[k, h] - far, acc)
        return acc

    to_ref[0] = table(bo_ref[...]) * LOG2E
    tp_ref[0] = table(bp_ref[...]) * LOG2E
    ts_ref[0] = table(bs_ref[...])


def _bias_tables(rel_bias):
    own, prev, samp = _bucket_tables()
    nh = rel_bias.shape[1]
    b = MOBA_BLOCK
    return pl.pallas_call(
        _bias_kernel,
        grid=(nh,),
        in_specs=[pl.BlockSpec(memory_space=pltpu.SMEM),
                  pl.BlockSpec((b, b), lambda h: (0, 0)),
                  pl.BlockSpec((b, b), lambda h: (0, 0)),
                  pl.BlockSpec((1, b + LANES), lambda h: (0, 0))],
        out_specs=[pl.BlockSpec((1, b, b), lambda h: (h, 0, 0)),
                   pl.BlockSpec((1, b, b), lambda h: (h, 0, 0)),
                   pl.BlockSpec((1, 1, b + LANES), lambda h: (h, 0, 0))],
        out_shape=[jax.ShapeDtypeStruct((nh, b, b), F32),
                   jax.ShapeDtypeStruct((nh, b, b), F32),
                   jax.ShapeDtypeStruct((nh, 1, b + LANES), F32)],
        compiler_params=_params(1),
        name="bias_tables",
    )(rel_bias, jnp.asarray(own.T), jnp.asarray(prev.T), jnp.asarray(samp))


def _inproj_kernel(n_t, x_ref, ada_ref, nw_ref, wq_ref, wh_ref, wg_ref, wt_ref, qa_ref, hg_ref, gt_ref, *t_refs):
    d = x_ref.shape[-1]
    ada = ada_ref[0]
    h = _rms_mod(x_ref[0], nw_ref[...], ada[:, 0:d], ada[:, d:2 * d]).astype(BF16)
    qa_ref[0] = jnp.dot(h, wq_ref[...], preferred_element_type=F32)
    hg_ref[0] = jnp.dot(h, wh_ref[...], preferred_element_type=F32)
    gt_ref[0] = jnp.dot(h, wg_ref[...], preferred_element_type=F32)
    t = lax.dot_general(wt_ref[...], h, (((1,), (1,)), ((), ())), preferred_element_type=F32)
    for i in range(n_t):
        ref = t_refs[i]
        ref[0] = t[i * ATTN_WIDTH:(i + 1) * ATTN_WIDTH].reshape(ref.shape[1:])


def _inproj(x, ada, nw, wq, wh, wg, wt, tm):
    b, s, d = x.shape
    r = ada.shape[1]
    n_t = wt.shape[0] // ATTN_WIDTH
    assert s % tm == 0 and r in (1, s) and (r == 1 or tm == s)
    t_shapes, t_specs = [], []
    for i in range(n_t):
        if i < 2:
            t_shapes.append(jax.ShapeDtypeStruct((b, ATTN_HEADS, ATTN_HEAD_DIM, s), F32))
            t_specs.append(pl.BlockSpec((1, ATTN_HEADS, ATTN_HEAD_DIM, tm), lambda i_, j_: (i_, 0, 0, j_)))
        else:
            t_shapes.append(jax.ShapeDtypeStruct((b, ATTN_WIDTH, s), F32))
            t_specs.append(pl.BlockSpec((1, ATTN_WIDTH, tm), lambda i_, j_: (i_, 0, j_)))
    row = lambda w: pl.BlockSpec((1, tm, w), lambda i_, j_: (i_, j_, 0))
    return pl.pallas_call(
        functools.partial(_inproj_kernel, n_t),
        grid=(b, s // tm),
        in_specs=[row(d),
                  pl.BlockSpec((1, r if r == 1 else tm, N_ADA * d), lambda i_, j_: (i_, 0, 0)),
                  _const_spec(nw.shape), _const_spec(wq.shape), _const_spec(wh.shape), _const_spec(wg.shape),
                  _const_spec(wt.shape)],
        out_specs=[row(wq.shape[1]), row(wh.shape[1]), row(wg.shape[1])] + t_specs,
        out_shape=[jax.ShapeDtypeStruct((b, s, wq.shape[1]), F32),
                   jax.ShapeDtypeStruct((b, s, wh.shape[1]), F32),
                   jax.ShapeDtypeStruct((b, s, wg.shape[1]), F32)] + t_shapes,
        compiler_params=_params(2),
        name="inproj",
    )(x, ada, nw, wq, wh, wg, wt)


def _attn_consts(s):
    nb = s // MOBA_BLOCK
    key_blk = np.arange(s) // MOBA_BLOCK
    gm = np.zeros((LANES, LANES), np.float32)
    for b in range(nb):
        for b2 in range(nb):
            gm[b, b * PAIR_SLOTS + b2] = 1.0
    er = np.zeros((s, LANES), np.float32)
    er[np.arange(s), key_blk] = 1.0
    return jnp.asarray(gm, BF16), jnp.asarray(er, BF16)


def _attn_prompt_kernel(spp, pt_ref, q_ref, kt_ref, vt_ref, to_ref, tp_ref, gm_ref, er_ref, qts_ref, ktn_ref, ts_ref, cache_ref,
                        o_ref, ps_ref, idx_ref, po_ref, kaug_ref, vaug_ref, qb_ref, sc_ref, pages_ref, sem):
    s_len = q_ref.shape[1]
    nb = s_len // MOBA_BLOCK
    blk = MOBA_BLOCK

    step = pl.program_id(0) * pl.num_programs(1) + pl.program_id(1)
    n_steps = pl.num_programs(0) * pl.num_programs(1)
    n_batches = pt_ref.shape[1] // PAGES_PER_STEP
    work = [(u, g) for u in range(spp) for g in range(n_batches)]
    n_ahead = min(PAGE_SLOTS - 1, len(work))

    def ring_slot(w, of_step):
        return lax.rem(of_step * len(work) + w, PAGE_SLOTS)

    def batch_copies(w, of_step, fixed_source=False):
        u, g = work[w]
        slot = ring_slot(w, of_step)
        n = jnp.minimum(of_step, n_steps - 1) * spp + u
        page = lambda j: 0 if fixed_source else pt_ref[n, g * PAGES_PER_STEP + j]
        return [pltpu.make_async_copy(cache_ref.at[page(j)], pages_ref.at[slot, j], sem.at[slot])
                for j in range(PAGES_PER_STEP)]

    def start_batch(w, of_step):
        for c in batch_copies(w, of_step):
            c.start()

    def finish_batch(w):
        u, g = work[w]
        n = step * spp + u
        for c in batch_copies(w, step, fixed_source=True):
            c.wait()
        nxt = w + n_ahead
        if nxt < len(work):
            start_batch(nxt, step)
        else:
            start_batch(nxt - len(work), step + 1)
        if g == 0:
            _sample_query(qts_ref, n, qb_ref)
        _page_scores(pages_ref.at[ring_slot(w, step)], qb_ref, sc_ref, g * PAGES_PER_STEP)
        if g == n_batches - 1:
            _select_blocks(sc_ref[...], n, qts_ref, ktn_ref, ts_ref, ps_ref, idx_ref, po_ref, u)

    order = list(range(nb - 1, -1, -1))
    due = [-1] + [min(nb - 1, ((w - 1) * max(1, nb // 2)) // len(work)) for w in range(1, len(work))]

    @pl.when(step == 0)
    def _():
        for w in range(n_ahead):
            start_batch(w, step)
    k_rows = kt_ref[0].reshape(PAIR, s_len).T
    kaug_ref[:, 0:PAIR] = k_rows.astype(BF16)
    kaug_ref[:, PAIR:2 * PAIR] = er_ref[...]
    vaug_ref[0:PAIR, :] = vt_ref[0].reshape(PAIR, s_len).astype(BF16)
    vaug_ref[PAIR:, :] = jnp.where(lax.broadcasted_iota(I32, (V_PAD_ROWS, s_len), 0) == 0, 1.0, 0.0).astype(BF16)
    q_t = q_ref[0].T * (SCALE * LOG2E)
    dim_row = lax.broadcasted_iota(I32, (PAIR, blk), 0)
    slot = lax.broadcasted_iota(I32, (LANES, 2 * blk), 0)
    t_own = jnp.concatenate([to_ref[0], to_ref[1]], axis=1)
    t_prev = jnp.concatenate([tp_ref[0], tp_ref[1]], axis=1)
    for w in range(len(work)):
        if due[w] < 0:
            finish_batch(w)

    q_hi, q_lo = [], []
    for j in range(nb):
        q2 = q_t[:, j * blk:(j + 1) * blk]
        qs = jnp.concatenate([jnp.where(dim_row < ATTN_HEAD_DIM, q2, 0.0), jnp.where(dim_row >= ATTN_HEAD_DIM, q2, 0.0)], axis=1)
        q_hi.append(qs.astype(BF16))
        q_lo.append((qs - q_hi[j].astype(F32)).astype(BF16) if j > MOBA_TOPK else None)

    mask_rows = [None] * nb
    if nb > MOBA_TOPK + 1:
        slot_k = lax.broadcasted_iota(I32, (LANES, PAIR), 0)
        dmean = jnp.zeros((LANES, PAIR), F32)
        for b in range(nb - 1):
            mb = jnp.sum(k_rows[b * blk:(b + 1) * blk], axis=0, keepdims=True) * (1.0 / blk)
            dmean = dmean + jnp.where(slot_k % PAIR_SLOTS == b, mb, 0.0) - jnp.where(slot_k // PAIR_SLOTS == b, mb, 0.0)
        d_hi = dmean.astype(BF16)
        d_lo = (dmean - d_hi.astype(F32)).astype(BF16)
        d_stack = jnp.concatenate([d_hi, d_hi, d_lo], axis=1)
        blk_b = slot // PAIR_SLOTS
        blk_b2 = slot % PAIR_SLOTS
        diffs = {j: jnp.dot(d_stack, jnp.concatenate([q_hi[j], q_lo[j], q_hi[j]], axis=0), preferred_element_type=F32)
                 for j in range(MOBA_TOPK + 1, nb)}
        ranks = {}
        for j, diff in diffs.items():
            beats = ((diff > 0.0) | ((diff == 0.0) & (blk_b2 < blk_b))) & (blk_b2 < j) & (blk_b < j)
            ranks[j] = jnp.dot(gm_ref[...], jnp.where(beats, 1.0, 0.0).astype(BF16), preferred_element_type=F32)
        for j, rank in ranks.items():
            keep = (slot == j) | ((slot < j) & (rank < MOBA_TOPK)) | (slot >= PAIR_SLOTS)
            mask_rows[j] = jnp.where(keep, 0.0, NEG).astype(BF16)

    def scores(j):
        n_keys = (j + 1) * blk
        if mask_rows[j] is None:
            return jnp.dot(kaug_ref[0:n_keys, 0:PAIR], q_hi[j], preferred_element_type=F32)
        return jnp.dot(kaug_ref[0:n_keys, :], jnp.concatenate([q_hi[j], mask_rows[j]], axis=0), preferred_element_type=F32)

    sc_next = scores(order[0])
    for pos, j in enumerate(order):
        n_keys = (j + 1) * blk
        sc = sc_next
        if pos + 1 < nb:
            sc_next = scores(order[pos + 1])
        pieces = []
        if j >= 2:
            pieces.append(sc[:n_keys - 2 * blk])
        if j >= 1:
            pieces.append(sc[n_keys - 2 * blk:n_keys - blk] + t_prev)
        pieces.append(sc[n_keys - blk:] + t_own)
        sc = jnp.concatenate(pieces, axis=0) if len(pieces) > 1 else pieces[0]
        m = jnp.max(sc, axis=0, keepdims=True)
        p = jnp.exp2(sc - m).astype(BF16)
        o2 = jnp.dot(vaug_ref[:, 0:n_keys], p, preferred_element_type=F32)
        o2 = o2[0:PAIR] * (1.0 / o2[PAIR:PAIR + 1])
        heads = jnp.concatenate([o2[:ATTN_HEAD_DIM, :blk], o2[ATTN_HEAD_DIM:, blk:]], axis=0)
        o_ref[0, j * blk:(j + 1) * blk, :] = heads.T
        for w in range(len(work)):
            if due[w] == pos:
                finish_batch(w)

    @pl.when(step == n_steps - 1)
    def _():
        for w in range(n_ahead):
            for c in batch_copies(w, step + 1, fixed_source=True):
                c.wait()


def _attention(qa, kt, vt, t_own, t_prev, page_table, qt_s, kt_new, t_samp, cache_kt):
    b, s, _ = qa.shape
    db, n_pages = page_table.shape
    n_pairs = ATTN_HEADS // 2
    past = n_pages * PAGE_SIZE
    assert s % MOBA_BLOCK == 0 and s // MOBA_BLOCK <= PAIR_SLOTS
    assert db % (b * n_pairs) == 0 and n_pages % PAGES_PER_STEP == 0
    assert past % MOBA_BLOCK == 0 and MOBA_TOPK <= past // MOBA_BLOCK <= LANES
    spp = db // (b * n_pairs)
    gm, er = _attn_consts(s)
    wsel = MOBA_TOPK * MOBA_BLOCK
    const = lambda shape: pl.BlockSpec(shape, lambda i, p, pt: (0,) * len(shape), pipeline_mode=pl.Buffered(1))
    per_sample = lambda w: pl.BlockSpec((spp, ATTN_HEADS, w), lambda i, p, pt: (i * n_pairs + p, 0, 0))
    return pl.pallas_call(
        functools.partial(_attn_prompt_kernel, spp),
        grid_spec=pltpu.PrefetchScalarGridSpec(
            num_scalar_prefetch=1,
            grid=(b, n_pairs),
            in_specs=[pl.BlockSpec((1, s, PAIR), lambda i, p, pt: (i, 0, p)),
                      pl.BlockSpec((1, 2, ATTN_HEAD_DIM, s), lambda i, p, pt: (i, p, 0, 0)),
                      pl.BlockSpec((1, 2, ATTN_HEAD_DIM, s), lambda i, p, pt: (i, p, 0, 0)),
                      pl.BlockSpec((2, MOBA_BLOCK, MOBA_BLOCK), lambda i, p, pt: (p, 0, 0)),
                      pl.BlockSpec((2, MOBA_BLOCK, MOBA_BLOCK), lambda i, p, pt: (p, 0, 0)),
                      const(gm.shape), const(er.shape), const(qt_s.shape), const(kt_new.shape), const(t_samp.shape),
                      pl.BlockSpec(memory_space=pl.ANY)],
            out_specs=[pl.BlockSpec((1, s, PAIR), lambda i, p, pt: (i, 0, p)),
                       per_sample(wsel), per_sample(LANES), per_sample(LANES)],
            scratch_shapes=[pltpu.VMEM((s, 2 * PAIR), BF16), pltpu.VMEM((PAIR + V_PAD_ROWS, s), BF16),
                            pltpu.VMEM((ATTN_WIDTH, PAGE_SIZE), F32), pltpu.VMEM((ATTN_HEADS, past), F32),
                            pltpu.VMEM((PAGE_SLOTS, PAGES_PER_STEP, ATTN_WIDTH, PAGE_SIZE), F32),
                            pltpu.SemaphoreType.DMA((PAGE_SLOTS,))]),
        out_shape=[jax.ShapeDtypeStruct((b, s, ATTN_WIDTH), F32),
                   jax.ShapeDtypeStruct((db, ATTN_HEADS, wsel), F32),
                   jax.ShapeDtypeStruct((db, ATTN_HEADS, LANES), I32),
                   jax.ShapeDtypeStruct((db, ATTN_HEADS, LANES), F32)],
        compiler_params=_params(2),
        name="attention",
    )(page_table, qa, kt, vt, t_own, t_prev, gm, er, qt_s, kt_new, t_samp, cache_kt)


def _lower_bound(logits, axis):
    m = jnp.max(logits, axis=axis, keepdims=True)
    e = jnp.exp(logits - m)
    first = e[0:1] if axis == 0 else e[:, 0:1]
    return first / jnp.sum(e, axis=axis, keepdims=True)


def _gated_out(o, gw, og):
    var = jnp.mean(o * o, axis=-1, keepdims=True)
    return (o * lax.rsqrt(var + EPS) * gw) * (og * _sigmoid(og))


def _cumsum_rows(tri, x):
    hi = x.astype(BF16)
    rest = x - hi.astype(F32)
    mid = rest.astype(BF16)
    lo = (rest - mid.astype(F32)).astype(BF16)
    n = x.shape[1]
    y = jnp.dot(tri, jnp.concatenate([hi, mid, lo], axis=1), preferred_element_type=F32)
    return y[:, :n] + y[:, n:2 * n] + y[:, 2 * n:]


def _hgrn_prompt_kernel(spp, pp_ref, q_ref, f_ref, v_ref, og_ref, lbl_ref, gw_ref, tri_ref, ps_ref, po_ref, vn_ref, cache_ref,
                        o_ref, s_ref, oa_ref, shift_s, vrow_s, vpages_ref, vsem):
    step = pl.program_id(0) * pl.num_programs(1) + pl.program_id(1)
    n_steps = pl.num_programs(0) * pl.num_programs(1)

    par = lax.rem(step, 2)

    @pl.when(step == 0)
    def _():
        for u in range(spp):
            for c in _value_copies(pp_ref, cache_ref, vpages_ref.at[0], vsem.at[0], u, u):
                c.start()

    @pl.when(step + 1 < n_steps)
    def _():
        for u in range(spp):
            for c in _value_copies(pp_ref, cache_ref, vpages_ref.at[1 - par], vsem.at[1 - par], (step + 1) * spp + u, u):
                c.start()

    for u in range(spp):
        for c in _value_copies(pp_ref, cache_ref, vpages_ref.at[par], vsem.at[par], None, u):
            c.wait()
        _weighted_values(vpages_ref.at[par], ps_ref, po_ref, vn_ref, oa_ref, u)

    t_len = q_ref.shape[1]
    n_sub = LANES // SUB
    half = SUB // 2
    dim = HGRN_DIM
    lb = _lower_bound(lbl_ref[...], 0)
    gw = gw_ref[...]
    tri = tri_ref[...]
    half_row = lax.broadcasted_iota(I32, (n_sub, half, dim), 1)
    trans_b = (((1,), (1,)), ((), ()))

    def prepare(t):
        rows = pl.ds(pl.multiple_of(t * LANES, LANES), LANES)
        qr = q_ref[0, rows, :]
        q = qr * _sigmoid(qr)
        f = lb + (1.0 - lb) * _sigmoid(f_ref[0, rows, :])
        v = v_ref[0, rows, :]
        cum = _cumsum_rows(tri, jnp.log2(f))
        return dict(rows=rows, q=q, kk=1.0 - f, v=v, vb=v.astype(BF16), cum=cum)

    def split(p):
        c3 = p["cum"].reshape(n_sub, SUB, dim)
        start = jnp.concatenate([jnp.zeros((1, 1, dim), F32), c3[:n_sub - 1, SUB - 1:SUB, :]], axis=0)
        p.update(c3=c3, start=start, b3=c3 - start, c_end=c3[n_sub - 1, SUB - 1:SUB, :],
                 q3=p["q"].reshape(n_sub, SUB, dim), k3=p["kk"].reshape(n_sub, SUB, dim))

    def first_products(p):
        p["upd"] = lax.dot_general(p["vb"], (p["kk"] * jnp.exp2(p["c_end"] - p["cum"])).astype(BF16),
                                   (((0,), (0,)), ((), ())), preferred_element_type=F32)
        qt3 = (p["q3"] * jnp.exp2(p["b3"])).astype(BF16)
        p["scores"] = []
        for i in range(1, n_sub):
            kt = (p["k3"][:i] * jnp.exp2(p["start"][i:i + 1] - p["c3"][:i])).reshape(i * SUB, dim).astype(BF16)
            p["scores"].append(lax.dot_general(qt3[i], kt, trans_b, preferred_element_type=F32).astype(BF16))

    def within_sub_chunks(p, slot):
        b3, q3 = p["b3"], p["q3"]
        shift_s[slot] = (b3 - jnp.log2(p["k3"])).reshape(LANES, dim)
        vrow_s[slot] = p["v"]
        key_row = lambda ref, s: jnp.concatenate(
            [jnp.broadcast_to(ref[slot, c * SUB + s:c * SUB + s + 1, :], (1, half, dim)) for c in range(n_sub)], axis=0)
        b_lo, b_hi, q_lo, q_hi = b3[:, :half], b3[:, half:], q3[:, :half], q3[:, half:]
        od_lo = jnp.zeros((n_sub, half, dim), F32)
        od_hi = jnp.zeros((n_sub, half, dim), F32)
        for s in range(SUB):
            bs, vs = key_row(shift_s, s), key_row(vrow_s, s)
            w_hi = jnp.sum(q_hi * jnp.exp2(b_hi - bs), axis=-1, keepdims=True)
            if s < half:
                w_lo = jnp.sum(q_lo * jnp.exp2(b_lo - bs), axis=-1, keepdims=True)
                od_lo = od_lo + jnp.where(half_row >= s, w_lo, 0.0) * vs
                od_hi = od_hi + w_hi * vs
            else:
                od_hi = od_hi + jnp.where(half_row >= s - half, w_hi, 0.0) * vs
        p["o_sub"] = jnp.concatenate([od_lo, od_hi], axis=1).reshape(LANES, dim)

    def tiles(g, st):
        group = [prepare(g * TILE_UNROLL + u) for u in range(TILE_UNROLL)]
        for p in group:
            split(p)
            first_products(p)
        for p in group:
            p["o_state"] = lax.dot_general((p["q"] * jnp.exp2(p["cum"])).astype(BF16), st.astype(BF16), trans_b,
                                           preferred_element_type=F32)
            st = st * jnp.exp2(p["c_end"]) + p["upd"]
        for u, p in enumerate(group):
            within_sub_chunks(p, u)
        for p in group:
            cross = [jnp.dot(a, p["vb"][0:(i + 1) * SUB], preferred_element_type=F32) for i, a in enumerate(p["scores"])]
            o = p["o_sub"] + p["o_state"] + jnp.concatenate([jnp.zeros((SUB, dim), F32)] + cross, axis=0)
            o_ref[0, p["rows"], :] = _gated_out(o, gw, og_ref[0, p["rows"], :])
        return st

    st = lax.fori_loop(0, t_len // (LANES * TILE_UNROLL), tiles, jnp.zeros((dim, dim), F32))
    s_ref[0, 0] = st.T


def _hgrn_tri():
    r = np.arange(LANES)
    return jnp.asarray((r[None, :] <= r[:, None]).astype(np.float32), BF16)


def _hgrn_prompt(hg, lb_logits, gw, phys, p_sel, p_own, v_new, cache_vt):
    b, s, _ = hg.shape
    db = p_sel.shape[0]
    assert s % (LANES * TILE_UNROLL) == 0 and db % (b * HGRN_HEADS) == 0
    spp = db // (b * HGRN_HEADS)
    col = lambda part: pl.BlockSpec((1, s, HGRN_DIM), lambda i, h, pp, part=part: (i, 0, part * HGRN_HEADS + h))
    const = lambda shape: pl.BlockSpec(shape, lambda i, h, pp: (0,) * len(shape), pipeline_mode=pl.Buffered(1))
    per_sample = lambda w: pl.BlockSpec((spp, ATTN_HEADS, 1, w), lambda i, h, pp: (i * HGRN_HEADS + h, 0, 0, 0))
    return pl.pallas_call(
        functools.partial(_hgrn_prompt_kernel, spp),
        grid_spec=pltpu.PrefetchScalarGridSpec(
            num_scalar_prefetch=1,
            grid=(b, HGRN_HEADS),
            in_specs=[col(0), col(1), col(2), col(3),
                      pl.BlockSpec((lb_logits.shape[0], HGRN_DIM), lambda i, h, pp: (0, h)),
                      const(gw.shape), const((LANES, LANES)),
                      per_sample(p_sel.shape[3]), per_sample(LANES), per_sample(ATTN_HEAD_DIM),
                      pl.BlockSpec(memory_space=pl.ANY)],
            out_specs=[pl.BlockSpec((1, s, HGRN_DIM), lambda i, h, pp: (i, 0, h)),
                       pl.BlockSpec((1, 1, HGRN_DIM, HGRN_DIM), lambda i, h, pp: (i, h, 0, 0)),
                       per_sample(ATTN_HEAD_DIM)],
            scratch_shapes=[pltpu.VMEM((TILE_UNROLL, LANES, HGRN_DIM), F32), pltpu.VMEM((TILE_UNROLL, LANES, HGRN_DIM), F32),
                            pltpu.VMEM((2, spp * N_SEL_TILES, ATTN_HEAD_DIM, PAGE_SIZE), F32),
                            pltpu.SemaphoreType.DMA((2, spp))]),
        out_shape=[jax.ShapeDtypeStruct((b, s, HGRN_WIDTH), F32),
                   jax.ShapeDtypeStruct((b, HGRN_HEADS, HGRN_DIM, HGRN_DIM), F32),
                   jax.ShapeDtypeStruct((db, ATTN_HEADS, 1, ATTN_HEAD_DIM), F32)],
        compiler_params=_params(2),
        name="hgrn_prompt",
    )(phys, hg, hg, hg, hg, lb_logits, gw, _hgrn_tri(), p_sel, p_own, v_new, cache_vt)


def _hgrn_sample_kernel(xt_ref, hg_ref, lblt_ref, gw_ref, st_ref, o_ref, so_ref):
    width = HGRN_WIDTH
    xt = xt_ref[0]
    lane = lax.broadcasted_iota(I32, xt.shape, 1)
    lb = _lower_bound(lblt_ref[...], 1)
    for u in range(st_ref.shape[0]):
        n = pl.program_id(0) * st_ref.shape[0] + u
        cols = jnp.sum(jnp.where(lane == n, xt, 0.0), axis=1, keepdims=True)
        hg = hg_ref[u]
        for h in range(HGRN_HEADS):
            rows = slice(h * HGRN_DIM, (h + 1) * HGRN_DIM)
            qr = cols[rows]
            q = qr * _sigmoid(qr)
            lbh = lb[rows]
            f = lbh + (1.0 - lbh) * _sigmoid(cols[width + h * HGRN_DIM:width + (h + 1) * HGRN_DIM])
            v = hg[:, 2 * width + h * HGRN_DIM:2 * width + (h + 1) * HGRN_DIM]
            og = hg[:, 3 * width + h * HGRN_DIM:3 * width + (h + 1) * HGRN_DIM]
            s_new = f * st_ref[u, h] + (1.0 - f) * v
            so_ref[u, h] = s_new
            o = jnp.sum(q * s_new, axis=0, keepdims=True)
            o_ref[u, :, rows] = _gated_out(o, gw_ref[...], og)


def _hgrn_sample(xt, hg, lb_logits_t, gw, state):
    db = state.shape[0]
    ns = SAMPLES_PER_STEP
    assert db % ns == 0
    return pl.pallas_call(
        _hgrn_sample_kernel,
        grid=(db // ns,),
        in_specs=[_const_spec(xt.shape),
                  pl.BlockSpec((ns, 1, hg.shape[2]), lambda n: (n, 0, 0)),
                  _const_spec(lb_logits_t.shape), _const_spec(gw.shape),
                  pl.BlockSpec((ns,) + state.shape[1:], lambda n: (n, 0, 0, 0))],
        out_specs=[pl.BlockSpec((ns, 1, HGRN_WIDTH), lambda n: (n, 0, 0)),
                   pl.BlockSpec((ns,) + state.shape[1:], lambda n: (n, 0, 0, 0))],
        out_shape=[jax.ShapeDtypeStruct((db, 1, HGRN_WIDTH), F32),
                   jax.ShapeDtypeStruct(state.shape, F32)],
        compiler_params=_params(1),
        name="hgrn_sample",
    )(xt, hg, lb_logits_t, gw, state)


def _post_kernel(x_ref, oa_ref, oh_ref, gt_ref, ada_ref, wua_ref, wuh_ref, wo_ref, nf_ref, w1_ref, w2_ref, fn_ref, y_ref):
    d = x_ref.shape[-1]
    ada = ada_ref[0]
    g = gt_ref[0]
    ua = jnp.dot(oa_ref[0].astype(BF16), wua_ref[...], preferred_element_type=F32)
    uh = jnp.dot(oh_ref[0].astype(BF16), wuh_ref[...], preferred_element_type=F32)
    merged = _sigmoid(g[:, :d]) * ua + _sigmoid(g[:, d:]) * uh
    x1 = x_ref[0] + ada[:, 2 * d:3 * d] * jnp.dot(merged.astype(BF16), wo_ref[...], preferred_element_type=F32)
    h2 = _rms_mod(x1, nf_ref[...], ada[:, 3 * d:4 * d], ada[:, 4 * d:5 * d]).astype(BF16)
    a = jnp.maximum(jnp.dot(h2, w1_ref[...], preferred_element_type=F32), 0.0)
    x2 = x1 + ada[:, 5 * d:6 * d] * jnp.dot((a * a).astype(BF16), w2_ref[...], preferred_element_type=F32)
    var = jnp.mean(x2 * x2, axis=-1, keepdims=True)
    y_ref[0] = x2 * lax.rsqrt(var + EPS) * fn_ref[...]


def _post(x, oa, oh, gt, ada, wua, wuh, wo, nf, w1, w2, fn, tm):
    b, s, d = x.shape
    r = ada.shape[1]
    assert s % tm == 0 and r in (1, s) and (r == 1 or tm == s)
    row = lambda w: pl.BlockSpec((1, tm, w), lambda i, j: (i, j, 0))
    return pl.pallas_call(
        _post_kernel,
        grid=(b, s // tm),
        in_specs=[row(d), row(oa.shape[2]), row(oh.shape[2]), row(gt.shape[2]),
                  pl.BlockSpec((1, r if r == 1 else tm, N_ADA * d), lambda i, j: (i, 0, 0)),
                  _const_spec(wua.shape), _const_spec(wuh.shape), _const_spec(wo.shape), _const_spec(nf.shape),
                  _const_spec(w1.shape), _const_spec(w2.shape), _const_spec(fn.shape)],
        out_specs=row(d),
        out_shape=jax.ShapeDtypeStruct((b, s, d), F32),
        compiler_params=_params(2),
        name="post",
    )(x, oa, oh, gt, ada, wua, wuh, wo, nf, w1, w2, fn)


def _sample_query(qt_ref, n, qb_ref):
    qt = qt_ref[0]
    pick = lax.broadcasted_iota(I32, qt.shape, 1) == n
    qb_ref[...] = jnp.broadcast_to(jnp.sum(jnp.where(pick, qt, 0.0), axis=1, keepdims=True), qb_ref.shape)


def _page_scores(pages_ref, qb_ref, sc_ref, first_page):
    for h in range(ATTN_HEADS):
        rows = slice(h * ATTN_HEAD_DIM, (h + 1) * ATTN_HEAD_DIM)
        qh = qb_ref[rows, :]
        for j in range(PAGES_PER_STEP):
            lanes = slice((first_page + j) * PAGE_SIZE, (first_page + j + 1) * PAGE_SIZE)
            sc_ref[h:h + 1, lanes] = jnp.sum(pages_ref[j, rows, :] * qh, axis=0, keepdims=True)


def _select_blocks(sc, n, qt_ref, kt_ref, ts_ref, ps_ref, idx_ref, po_ref, u):
    n_blk = sc.shape[1] // MOBA_BLOCK
    lane = lax.broadcasted_iota(I32, (ATTN_HEADS, LANES), 1).astype(F32)
    bs = jnp.full((ATTN_HEADS, LANES), -jnp.inf, F32)
    for b in range(n_blk):
        tot = jnp.sum(sc[:, b * MOBA_BLOCK:(b + 1) * MOBA_BLOCK], axis=1, keepdims=True)
        bs = jnp.where(lane == float(b), tot * (1.0 / MOBA_BLOCK), bs)
    picks = []
    for _ in range(MOBA_TOPK):
        m = jnp.max(bs, axis=1, keepdims=True)
        i = jnp.min(jnp.where(bs == m, lane, float(LANES)), axis=1, keepdims=True)
        picks.append(i)
        bs = jnp.where(lane == i, -jnp.inf, bs)
    ts = ts_ref[...].reshape(ATTN_HEADS, MOBA_BLOCK + LANES)
    logits = []
    for i in picks:
        blk = jnp.zeros((ATTN_HEADS, MOBA_BLOCK), F32)
        for b in range(n_blk):
            blk = jnp.where(i == float(b), sc[:, b * MOBA_BLOCK:(b + 1) * MOBA_BLOCK], blk)
        logits.append(blk * SCALE + jnp.where(i == float(n_blk - 1), ts[:, :MOBA_BLOCK], 0.0))
    lg = jnp.concatenate(logits, axis=1)
    prod = qt_ref[0] * kt_ref[0].reshape(ATTN_WIDTH, -1)
    own_all = jnp.sum(prod.reshape(ATTN_HEADS, ATTN_HEAD_DIM, prod.shape[1]), axis=1)
    pick = lax.broadcasted_iota(I32, own_all.shape, 1) == n
    own = jnp.sum(jnp.where(pick, own_all, 0.0), axis=1, keepdims=True) * SCALE + ts[:, MOBA_BLOCK:MOBA_BLOCK + 1]
    m = jnp.maximum(jnp.max(lg, axis=1, keepdims=True), own)
    e = jnp.exp(lg - m)
    eo = jnp.exp(own - m)
    den = jnp.sum(e, axis=1, keepdims=True) + eo
    ps_ref[u] = e / den
    po_ref[u] = jnp.broadcast_to(eo / den, (ATTN_HEADS, LANES))
    idx = jnp.zeros((ATTN_HEADS, LANES), F32)
    for t, i in enumerate(picks):
        idx = jnp.where(lane == float(t), i, idx)
    idx_ref[u] = idx.astype(I32)


N_SEL_PAGES = MOBA_TOPK * (MOBA_BLOCK // PAGE_SIZE)


N_SEL_TILES = ATTN_HEADS * N_SEL_PAGES


def _value_copies(pp_ref, cache_ref, vpages_ref, sem, n, u):
    page = lambda h, j: 0 if n is None else pp_ref[(n * ATTN_HEADS + h) * N_SEL_PAGES + j]
    return [pltpu.make_async_copy(cache_ref.at[page(h, j), h],
                                  vpages_ref.at[u * N_SEL_TILES + h * N_SEL_PAGES + j], sem.at[u])
            for h in range(ATTN_HEADS) for j in range(N_SEL_PAGES)]


def _weighted_values(vpages_ref, ps_ref, po_ref, vn_ref, o_ref, u):
    acc = []
    for h in range(ATTN_HEADS):
        ps = ps_ref[u, h]
        a = jnp.zeros((ATTN_HEAD_DIM, PAGE_SIZE), F32)
        for j in range(N_SEL_PAGES):
            a = a + vpages_ref[u * N_SEL_TILES + h * N_SEL_PAGES + j] * ps[:, j * PAGE_SIZE:(j + 1) * PAGE_SIZE]
        acc.append(a)
    o = lax.dot_general(jnp.ones((SUBLANES, PAGE_SIZE), F32), jnp.concatenate(acc, axis=0), (((1,), (1,)), ((), ())),
                        preferred_element_type=F32, precision=HIGHEST)
    for h in range(ATTN_HEADS):
        o_ref[u, h] = (o[0:1, h * ATTN_HEAD_DIM:(h + 1) * ATTN_HEAD_DIM]
                       + po_ref[u, h][:, 0:ATTN_HEAD_DIM] * vn_ref[u, h])


def kernel(x_prompt, x_sample, cache_k, cache_v, state_hgrn, page_table, c_prompt, c_sample, rel_bias, hgrn_lb_logits,
           w_ada, b_ada, norm_mix_w, w_in, hgrn_gnorm_w, w_up_attn, w_up_hgrn, w_out, norm_ffn_w, w_ff1, w_ff2, final_norm_w):
    depth = w_in.shape[0]
    assert depth == 1, "one-layer trunk"
    b, s, d = x_prompt.shape
    db, ds, _ = x_sample.shape
    assert ds == 1
    past = page_table.shape[1] * PAGE_SIZE

    w = w_in[0]
    a0, a1, a2 = ATTN_WIDTH, 2 * ATTN_WIDTH, 3 * ATTN_WIDTH
    h0 = a2 + 4 * HGRN_WIDTH
    wq = w[:, :a0].astype(BF16)
    wh = w[:, a2:h0].astype(BF16)
    wg = w[:, h0:].astype(BF16)
    wt_kv = w[:, a0:a2].T.astype(BF16)
    wt_s = jnp.concatenate([w[:, a0:a2], w[:, :a0], w[:, a2:a2 + 2 * HGRN_WIDTH]], axis=1).T.astype(BF16)
    wua, wuh, wo = w_up_attn[0].astype(BF16), w_up_hgrn[0].astype(BF16), w_out[0].astype(BF16)
    w1, w2 = w_ff1[0].astype(BF16), w_ff2[0].astype(BF16)
    fn = final_norm_w.reshape(1, d)

    ada = _ada(jnp.concatenate([c_prompt, c_sample], axis=0), w_ada[0], b_ada)
    ada_p = ada[:b].reshape(b, 1, N_ADA * d)
    ada_s = ada[b:].reshape(1, db, N_ADA * d)
    t_own, t_prev, t_samp = _bias_tables(rel_bias)

    xs = x_sample.reshape(1, db, d)
    qa_s, hg_s, gt_s, kt_s, vt_s, qt_s, qht_s, fht_s = _inproj(xs, ada_s, norm_mix_w, wq, wh, wg, wt_s, db)
    qa, hg, gt, kt, vt = _inproj(x_prompt, ada_p, norm_mix_w, wq, wh, wg, wt_kv, ROW_TILE)
    cache_kt = jnp.transpose(cache_k[0], (0, 2, 3, 1))
    cache_vt = jnp.transpose(cache_v[0], (0, 2, 3, 1))

    o_attn, p_sel, idx, p_own = _attention(qa, kt, vt, t_own, t_prev, page_table, qt_s, kt_s, t_samp,
                                           cache_kt.reshape(-1, ATTN_WIDTH, PAGE_SIZE))
    ppb = MOBA_BLOCK // PAGE_SIZE
    logical = idx[:, :, :MOBA_TOPK, None] * ppb + jnp.arange(ppb, dtype=I32)
    phys = jnp.take_along_axis(page_table, logical.reshape(db, -1), axis=1).reshape(-1)
    v_new = jnp.transpose(vt_s[0], (2, 0, 1))[:, :, None, :]
    o_hgrn, state_p, o_attn_s = _hgrn_prompt(hg, hgrn_lb_logits, hgrn_gnorm_w, phys, p_sel[:, :, None, :],
                                             p_own[:, :, None, :], v_new, cache_vt)
    y_prompt = _post(x_prompt, o_attn, o_hgrn, gt, ada_p, wua, wuh, wo, norm_ffn_w, w1, w2, fn, ROW_TILE)

    o_attn_s = o_attn_s.reshape(1, db, ATTN_WIDTH)
    xt = jnp.concatenate([qht_s, fht_s], axis=1)
    o_hgrn_s, state_s = _hgrn_sample(xt, hg_s.reshape(db, 1, -1), hgrn_lb_logits.T, hgrn_gnorm_w, state_hgrn[0])
    y_sample = _post(xs, o_attn_s, o_hgrn_s.reshape(1, db, HGRN_WIDTH), gt_s, ada_s, wua, wuh, wo, norm_ffn_w, w1, w2, fn, db)

    to_rows = lambda t: jnp.transpose(t, (0, 3, 1, 2))[None]
    return (y_prompt, y_sample.reshape(db, 1, d), to_rows(kt), to_rows(vt), state_p[None],
            to_rows(kt_s).reshape(1, db, 1, ATTN_HEADS, ATTN_HEAD_DIM), to_rows(vt_s).reshape(1, db, 1, ATTN_HEADS, ATTN_HEAD_DIM),
            state_s[None])
```

```python
import functools
import math

import numpy as np
import jax
import jax.numpy as jnp
from jax import lax
from jax.experimental import pallas as pl
from jax.experimental.pallas import tpu as pltpu

F32, BF16, I32 = jnp.float32, jnp.bfloat16, jnp.int32
HIGHEST = lax.Precision.HIGHEST

ATTN_HEADS = 8
ATTN_HEAD_DIM = 64
ATTN_WIDTH = ATTN_HEADS * ATTN_HEAD_DIM
MOBA_BLOCK = 256
MOBA_TOPK = 3
PAGE_SIZE = 128
N_BUCKETS = 32
MAX_DISTANCE = 128
HGRN_HEADS = 4
HGRN_DIM = 128
HGRN_WIDTH = HGRN_HEADS * HGRN_DIM
N_ADA = 6
EPS = 1e-6
SCALE = ATTN_HEAD_DIM ** -0.5
LOG2E = math.log2(math.e)

LANES = 128
SUBLANES = 8
VMEM_BYTES_V7X = 64 * 1024 * 1024
VMEM_LIMIT = VMEM_BYTES_V7X * 7 // 8

NEG = -1e30
SUB = 16
TILE_UNROLL = 4
ROW_TILE = 256
INPROJ_ROW_TILE = 512
PAGES_PER_STEP = 16
PAGE_SLOTS = 3
SAMPLES_PER_STEP = 8
PAIR = 2 * ATTN_HEAD_DIM
PAIR_SLOTS = 8
V_PAD_ROWS = 16


def _params(n_grid):
    return pltpu.CompilerParams(dimension_semantics=("arbitrary",) * n_grid, vmem_limit_bytes=VMEM_LIMIT)


def _const_spec(shape):
    nd = len(shape)
    return pl.BlockSpec(shape, lambda *_: (0,) * nd, pipeline_mode=pl.Buffered(1))


def _sigmoid(x):
    return jax.nn.sigmoid(x)


def _rms_mod(x, w, shift, scale):
    var = jnp.mean(x * x, axis=-1, keepdims=True)
    return (x * lax.rsqrt(var + EPS) * w) * (1.0 + scale) + shift


def _ada_kernel(c_ref, w_ref, b_ref, o_ref):
    c = c_ref[...]
    o_ref[...] = jnp.dot(c * _sigmoid(c), w_ref[...], preferred_element_type=F32, precision=HIGHEST) + b_ref[...]


def _ada(c_all, w_ada, b_ada):
    n, d = c_all.shape
    width = w_ada.shape[1]
    tn = 1536
    assert width % tn == 0
    return pl.pallas_call(
        _ada_kernel,
        grid=(width // tn,),
        in_specs=[pl.BlockSpec((n, d), lambda j: (0, 0)),
                  pl.BlockSpec((d, tn), lambda j: (0, j)),
                  pl.BlockSpec((1, tn), lambda j: (0, j))],
        out_specs=pl.BlockSpec((n, tn), lambda j: (0, j)),
        out_shape=jax.ShapeDtypeStruct((n, width), F32),
        compiler_params=_params(1),
        name="ada",
    )(c_all, w_ada, b_ada)


def _bucket_np(dist):
    n = np.maximum(dist, 0)
    max_exact = N_BUCKETS // 2
    nf = np.maximum(n, max_exact).astype(np.float32)
    large = max_exact + (np.log(nf / np.float32(max_exact)) / np.float32(math.log(MAX_DISTANCE / max_exact))
                         * np.float32(N_BUCKETS - max_exact)).astype(np.int32)
    large = np.minimum(large, N_BUCKETS - 1)
    return np.where(n < max_exact, n, large).astype(np.int32)


def _bucket_tables():
    i = np.arange(MOBA_BLOCK)[:, None]
    j = np.arange(MOBA_BLOCK)[None, :]
    own = np.where(j <= i, _bucket_np(i - j), -1).astype(np.int32)
    prev = _bucket_np(MOBA_BLOCK + i - j)
    assert int(_bucket_np(np.array([MOBA_BLOCK + 1]))[0]) == N_BUCKETS - 1
    samp = np.full((1, MOBA_BLOCK + LANES), N_BUCKETS - 1, np.int32)
    samp[0, :MOBA_BLOCK] = _bucket_np(MOBA_BLOCK - np.arange(MOBA_BLOCK))
    samp[0, MOBA_BLOCK] = 0
    return own, prev, samp


def _bias_kernel(rb_ref, bo_ref, bp_ref, bs_ref, to_ref, tp_ref, ts_ref):
    h = pl.program_id(0)
    far = rb_ref[N_BUCKETS - 1, h]

    def table(bucket):
        acc = jnp.where(bucket < 0, NEG, 0.0).astype(F32)
        for k in range(N_BUCKETS - 1):
            acc = jnp.where(bucket == k, rb_ref[k, h] - far, acc)
        return acc

    to_ref[0] = table(bo_ref[...]) * LOG2E
    tp_ref[0] = table(bp_ref[...]) * LOG2E
    ts_ref[0] = table(bs_ref[...])


def _bias_tables(rel_bias):
    own, prev, samp = _bucket_tables()
    nh = rel_bias.shape[1]
    b = MOBA_BLOCK
    return pl.pallas_call(
        _bias_kernel,
        grid=(nh,),
        in_specs=[pl.BlockSpec(memory_space=pltpu.SMEM),
                  pl.BlockSpec((b, b), lambda h: (0, 0)),
                  pl.BlockSpec((b, b), lambda h: (0, 0)),
                  pl.BlockSpec((1, b + LANES), lambda h: (0, 0))],
        out_specs=[pl.BlockSpec((1, b, b), lambda h: (h, 0, 0)),
                   pl.BlockSpec((1, b, b), lambda h: (h, 0, 0)),
                   pl.BlockSpec((1, 1, b + LANES), lambda h: (h, 0, 0))],
        out_shape=[jax.ShapeDtypeStruct((nh, b, b), F32),
                   jax.ShapeDtypeStruct((nh, b, b), F32),
                   jax.ShapeDtypeStruct((nh, 1, b + LANES), F32)],
        compiler_params=_params(1),
        name="bias_tables",
    )(rel_bias, jnp.asarray(own.T), jnp.asarray(prev.T), jnp.asarray(samp))


def _inproj_kernel(n_t, x_ref, ada_ref, nw_ref, wq_ref, wh_ref, wg_ref, wt_ref, qa_ref, hg_ref, gt_ref, *t_refs):
    d = x_ref.shape[-1]
    ada = ada_ref[0]
    h = _rms_mod(x_ref[0], nw_ref[...], ada[:, 0:d], ada[:, d:2 * d]).astype(BF16)
    qa_ref[0] = jnp.dot(h, wq_ref[...], preferred_element_type=F32)
    hg_ref[0] = jnp.dot(h, wh_ref[...], preferred_element_type=F32)
    gt_ref[0] = jnp.dot(h, wg_ref[...], preferred_element_type=F32)
    t = lax.dot_general(wt_ref[...], h, (((1,), (1,)), ((), ())), preferred_element_type=F32)
    for i in range(n_t):
        ref = t_refs[i]
        ref[0] = t[i * ATTN_WIDTH:(i + 1) * ATTN_WIDTH].reshape(ref.shape[1:])


def _inproj(x, ada, nw, wq, wh, wg, wt, tm):
    b, s, d = x.shape
    r = ada.shape[1]
    n_t = wt.shape[0] // ATTN_WIDTH
    assert s % tm == 0 and r in (1, s) and (r == 1 or tm == s)
    t_shapes, t_specs = [], []
    for i in range(n_t):
        if i < 2:
            t_shapes.append(jax.ShapeDtypeStruct((b, ATTN_HEADS, ATTN_HEAD_DIM, s), F32))
            t_specs.append(pl.BlockSpec((1, ATTN_HEADS, ATTN_HEAD_DIM, tm), lambda i_, j_: (i_, 0, 0, j_)))
        else:
            t_shapes.append(jax.ShapeDtypeStruct((b, ATTN_WIDTH, s), F32))
            t_specs.append(pl.BlockSpec((1, ATTN_WIDTH, tm), lambda i_, j_: (i_, 0, j_)))
    row = lambda w: pl.BlockSpec((1, tm, w), lambda i_, j_: (i_, j_, 0))
    return pl.pallas_call(
        functools.partial(_inproj_kernel, n_t),
        grid=(b, s // tm),
        in_specs=[row(d),
                  pl.BlockSpec((1, r if r == 1 else tm, N_ADA * d), lambda i_, j_: (i_, 0, 0)),
                  _const_spec(nw.shape), _const_spec(wq.shape), _const_spec(wh.shape), _const_spec(wg.shape),
                  _const_spec(wt.shape)],
        out_specs=[row(wq.shape[1]), row(wh.shape[1]), row(wg.shape[1])] + t_specs,
        out_shape=[jax.ShapeDtypeStruct((b, s, wq.shape[1]), F32),
                   jax.ShapeDtypeStruct((b, s, wh.shape[1]), F32),
                   jax.ShapeDtypeStruct((b, s, wg.shape[1]), F32)] + t_shapes,
        compiler_params=_params(2),
        name="inproj",
    )(x, ada, nw, wq, wh, wg, wt)


def _attn_consts(s):
    nb = s // MOBA_BLOCK
    key_blk = np.arange(s) // MOBA_BLOCK
    gm = np.zeros((LANES, LANES), np.float32)
    for b in range(nb):
        for b2 in range(nb):
            gm[b, b * PAIR_SLOTS + b2] = 1.0
    er = np.zeros((s, LANES), np.float32)
    er[np.arange(s), key_blk] = 1.0
    return jnp.asarray(gm, BF16), jnp.asarray(er, BF16)


def _attn_prompt_kernel(spp, pt_ref, q_ref, kt_ref, vt_ref, to_ref, tp_ref, gm_ref, er_ref, qts_ref, ktn_ref, ts_ref, cache_ref,
                        o_ref, ps_ref, idx_ref, po_ref, kaug_ref, vaug_ref, qb_ref, sc_ref, pages_ref, sem):
    s_len = q_ref.shape[1]
    nb = s_len // MOBA_BLOCK
    blk = MOBA_BLOCK

    step = pl.program_id(0) * pl.num_programs(1) + pl.program_id(1)
    n_steps = pl.num_programs(0) * pl.num_programs(1)
    n_batches = pt_ref.shape[1] // PAGES_PER_STEP
    work = [(u, g) for u in range(spp) for g in range(n_batches)]
    n_ahead = min(PAGE_SLOTS - 1, len(work))

    def ring_slot(w, of_step):
        return lax.rem(of_step * len(work) + w, PAGE_SLOTS)

    def batch_copies(w, of_step, fixed_source=False):
        u, g = work[w]
        slot = ring_slot(w, of_step)
        n = jnp.minimum(of_step, n_steps - 1) * spp + u
        page = lambda j: 0 if fixed_source else pt_ref[n, g * PAGES_PER_STEP + j]
        return [pltpu.make_async_copy(cache_ref.at[page(j)], pages_ref.at[slot, j], sem.at[slot])
                for j in range(PAGES_PER_STEP)]

    def start_batch(w, of_step):
        for c in batch_copies(w, of_step):
            c.start()

    def finish_batch(w):
        u, g = work[w]
        n = step * spp + u
        for c in batch_copies(w, step, fixed_source=True):
            c.wait()
        nxt = w + n_ahead
        if nxt < len(work):
            start_batch(nxt, step)
        else:
            start_batch(nxt - len(work), step + 1)
        if g == 0:
            _sample_query(qts_ref, n, qb_ref)
        _page_scores(pages_ref.at[ring_slot(w, step)], qb_ref, sc_ref, g * PAGES_PER_STEP)
        if g == n_batches - 1:
            _select_blocks(sc_ref[...], n, qts_ref, ktn_ref, ts_ref, ps_ref, idx_ref, po_ref, u)

    order = list(range(nb - 1, -1, -1))
    due = [-1] + [min(nb - 1, ((w - 1) * max(1, nb // 2)) // len(work)) for w in range(1, len(work))]

    @pl.when(step == 0)
    def _():
        for w in range(n_ahead):
            start_batch(w, step)
    k_rows = kt_ref[0].reshape(PAIR, s_len).T
    kaug_ref[:, 0:PAIR] = k_rows.astype(BF16)
    kaug_ref[:, PAIR:2 * PAIR] = er_ref[...]
    vaug_ref[0:PAIR, :] = vt_ref[0].reshape(PAIR, s_len).astype(BF16)
    vaug_ref[PAIR:, :] = jnp.where(lax.broadcasted_iota(I32, (V_PAD_ROWS, s_len), 0) == 0, 1.0, 0.0).astype(BF16)
    q_t = q_ref[0].T * (SCALE * LOG2E)
    dim_row = lax.broadcasted_iota(I32, (PAIR, blk), 0)
    slot = lax.broadcasted_iota(I32, (LANES, 2 * blk), 0)
    t_own = jnp.concatenate([to_ref[0], to_ref[1]], axis=1)
    t_prev = jnp.concatenate([tp_ref[0], tp_ref[1]], axis=1)
    for w in range(len(work)):
        if due[w] < 0:
            finish_batch(w)

    q_hi, q_lo = [], []
    for j in range(nb):
        q2 = q_t[:, j * blk:(j + 1) * blk]
        qs = jnp.concatenate([jnp.where(dim_row < ATTN_HEAD_DIM, q2, 0.0), jnp.where(dim_row >= ATTN_HEAD_DIM, q2, 0.0)], axis=1)
        q_hi.append(qs.astype(BF16))
        q_lo.append((qs - q_hi[j].astype(F32)).astype(BF16) if j > MOBA_TOPK else None)

    mask_rows = [None] * nb
    if nb > MOBA_TOPK + 1:
        slot_k = lax.broadcasted_iota(I32, (LANES, PAIR), 0)
        dmean = jnp.zeros((LANES, PAIR), F32)
        for b in range(nb - 1):
            mb = jnp.sum(k_rows[b * blk:(b + 1) * blk], axis=0, keepdims=True) * (1.0 / blk)
            dmean = dmean + jnp.where(slot_k % PAIR_SLOTS == b, mb, 0.0) - jnp.where(slot_k // PAIR_SLOTS == b, mb, 0.0)
        d_hi = dmean.astype(BF16)
        d_lo = (dmean - d_hi.astype(F32)).astype(BF16)
        d_stack = jnp.concatenate([d_hi, d_hi, d_lo], axis=1)
        blk_b = slot // PAIR_SLOTS
        blk_b2 = slot % PAIR_SLOTS
        diffs = {j: jnp.dot(d_stack, jnp.concatenate([q_hi[j], q_lo[j], q_hi[j]], axis=0), preferred_element_type=F32)
                 for j in range(MOBA_TOPK + 1, nb)}
        ranks = {}
        for j, diff in diffs.items():
            beats = ((diff > 0.0) | ((diff == 0.0) & (blk_b2 < blk_b))) & (blk_b2 < j) & (blk_b < j)
            ranks[j] = jnp.dot(gm_ref[...], jnp.where(beats, 1.0, 0.0).astype(BF16), preferred_element_type=F32)
        for j, rank in ranks.items():
            keep = (slot == j) | ((slot < j) & (rank < MOBA_TOPK)) | (slot >= PAIR_SLOTS)
            mask_rows[j] = jnp.where(keep, 0.0, NEG).astype(BF16)

    def scores(j):
        n_keys = (j + 1) * blk
        if mask_rows[j] is None:
            return jnp.dot(kaug_ref[0:n_keys, 0:PAIR], q_hi[j], preferred_element_type=F32)
        return jnp.dot(kaug_ref[0:n_keys, :], jnp.concatenate([q_hi[j], mask_rows[j]], axis=0), preferred_element_type=F32)

    sc_next = scores(order[0])
    for pos, j in enumerate(order):
        n_keys = (j + 1) * blk
        sc = sc_next
        if pos + 1 < nb:
            sc_next = scores(order[pos + 1])
        pieces = []
        if j >= 2:
            pieces.append(sc[:n_keys - 2 * blk])
        if j >= 1:
            pieces.append(sc[n_keys - 2 * blk:n_keys - blk] + t_prev)
        pieces.append(sc[n_keys - blk:] + t_own)
        sc = jnp.concatenate(pieces, axis=0) if len(pieces) > 1 else pieces[0]
        m = jnp.max(sc, axis=0, keepdims=True)
        p = jnp.exp2(sc - m).astype(BF16)
        o2 = jnp.dot(vaug_ref[:, 0:n_keys], p, preferred_element_type=F32)
        o2 = o2[0:PAIR] * (1.0 / o2[PAIR:PAIR + 1])
        heads = jnp.concatenate([o2[:ATTN_HEAD_DIM, :blk], o2[ATTN_HEAD_DIM:, blk:]], axis=0)
        o_ref[0, j * blk:(j + 1) * blk, :] = heads.T
        for w in range(len(work)):
            if due[w] == pos:
                finish_batch(w)

    @pl.when(step == n_steps - 1)
    def _():
        for w in range(n_ahead):
            for c in batch_copies(w, step + 1, fixed_source=True):
                c.wait()


def _attention(qa, kt, vt, t_own, t_prev, page_table, qt_s, kt_new, t_samp, cache_kt):
    b, s, _ = qa.shape
    db, n_pages = page_table.shape
    n_pairs = ATTN_HEADS // 2
    past = n_pages * PAGE_SIZE
    assert s % MOBA_BLOCK == 0 and s // MOBA_BLOCK <= PAIR_SLOTS
    assert db % (b * n_pairs) == 0 and n_pages % PAGES_PER_STEP == 0
    assert past % MOBA_BLOCK == 0 and MOBA_TOPK <= past // MOBA_BLOCK <= LANES
    spp = db // (b * n_pairs)
    gm, er = _attn_consts(s)
    wsel = MOBA_TOPK * MOBA_BLOCK
    const = lambda shape: pl.BlockSpec(shape, lambda i, p, pt: (0,) * len(shape), pipeline_mode=pl.Buffered(1))
    per_sample = lambda w: pl.BlockSpec((spp, ATTN_HEADS, w), lambda i, p, pt: (i * n_pairs + p, 0, 0))
    return pl.pallas_call(
        functools.partial(_attn_prompt_kernel, spp),
        grid_spec=pltpu.PrefetchScalarGridSpec(
            num_scalar_prefetch=1,
            grid=(b, n_pairs),
            in_specs=[pl.BlockSpec((1, s, PAIR), lambda i, p, pt: (i, 0, p)),
                      pl.BlockSpec((1, 2, ATTN_HEAD_DIM, s), lambda i, p, pt: (i, p, 0, 0)),
                      pl.BlockSpec((1, 2, ATTN_HEAD_DIM, s), lambda i, p, pt: (i, p, 0, 0)),
                      pl.BlockSpec((2, MOBA_BLOCK, MOBA_BLOCK), lambda i, p, pt: (p, 0, 0)),
                      pl.BlockSpec((2, MOBA_BLOCK, MOBA_BLOCK), lambda i, p, pt: (p, 0, 0)),
                      const(gm.shape), const(er.shape), const(qt_s.shape), const(kt_new.shape), const(t_samp.shape),
                      pl.BlockSpec(memory_space=pl.ANY)],
            out_specs=[pl.BlockSpec((1, s, PAIR), lambda i, p, pt: (i, 0, p)),
                       per_sample(wsel), per_sample(LANES), per_sample(LANES)],
            scratch_shapes=[pltpu.VMEM((s, 2 * PAIR), BF16), pltpu.VMEM((PAIR + V_PAD_ROWS, s), BF16),
                            pltpu.VMEM((ATTN_WIDTH, PAGE_SIZE), F32), pltpu.VMEM((ATTN_HEADS, past), F32),
                            pltpu.VMEM((PAGE_SLOTS, PAGES_PER_STEP, ATTN_WIDTH, PAGE_SIZE), F32),
                            pltpu.SemaphoreType.DMA((PAGE_SLOTS,))]),
        out_shape=[jax.ShapeDtypeStruct((b, s, ATTN_WIDTH), F32),
                   jax.ShapeDtypeStruct((db, ATTN_HEADS, wsel), F32),
                   jax.ShapeDtypeStruct((db, ATTN_HEADS, LANES), I32),
                   jax.ShapeDtypeStruct((db, ATTN_HEADS, LANES), F32)],
        compiler_params=_params(2),
        name="attention",
    )(page_table, qa, kt, vt, t_own, t_prev, gm, er, qt_s, kt_new, t_samp, cache_kt)


def _lower_bound(logits, axis):
    m = jnp.max(logits, axis=axis, keepdims=True)
    e = jnp.exp(logits - m)
    first = e[0:1] if axis == 0 else e[:, 0:1]
    return first / jnp.sum(e, axis=axis, keepdims=True)


def _gated_out(o, gw, og):
    var = jnp.mean(o * o, axis=-1, keepdims=True)
    return (o * lax.rsqrt(var + EPS) * gw) * (og * _sigmoid(og))


def _cumsum_rows(tri, x):
    hi = x.astype(BF16)
    rest = x - hi.astype(F32)
    mid = rest.astype(BF16)
    lo = (rest - mid.astype(F32)).astype(BF16)
    n = x.shape[1]
    y = jnp.dot(tri, jnp.concatenate([hi, mid, lo], axis=1), preferred_element_type=F32)
    return y[:, :n] + y[:, n:2 * n] + y[:, 2 * n:]


def _hgrn_prompt_kernel(spp, pp_ref, q_ref, f_ref, v_ref, og_ref, lbl_ref, gw_ref, tri_ref, ps_ref, po_ref, vn_ref, cache_ref,
                        o_ref, s_ref, oa_ref, shift_s, vrow_s, vpages_ref, vsem):
    step = pl.program_id(0) * pl.num_programs(1) + pl.program_id(1)
    n_steps = pl.num_programs(0) * pl.num_programs(1)

    par = lax.rem(step, 2)

    @pl.when(step == 0)
    def _():
        for u in range(spp):
            for c in _value_copies(pp_ref, cache_ref, vpages_ref.at[0], vsem.at[0], u, u):
                c.start()

    @pl.when(step + 1 < n_steps)
    def _():
        for u in range(spp):
            for c in _value_copies(pp_ref, cache_ref, vpages_ref.at[1 - par], vsem.at[1 - par], (step + 1) * spp + u, u):
                c.start()

    for u in range(spp):
        for c in _value_copies(pp_ref, cache_ref, vpages_ref.at[par], vsem.at[par], None, u):
            c.wait()
        _weighted_values(vpages_ref.at[par], ps_ref, po_ref, vn_ref, oa_ref, u)

    t_len = q_ref.shape[1]
    n_sub = LANES // SUB
    half = SUB // 2
    dim = HGRN_DIM
    lb = _lower_bound(lbl_ref[...], 0)
    gw = gw_ref[...]
    tri = tri_ref[...]
    half_row = lax.broadcasted_iota(I32, (n_sub, half, dim), 1)
    trans_b = (((1,), (1,)), ((), ()))

    def prepare(t):
        rows = pl.ds(pl.multiple_of(t * LANES, LANES), LANES)
        qr = q_ref[0, rows, :]
        q = qr * _sigmoid(qr)
        f = lb + (1.0 - lb) * _sigmoid(f_ref[0, rows, :])
        v = v_ref[0, rows, :]
        cum = _cumsum_rows(tri, jnp.log2(f))
        return dict(rows=rows, q=q, kk=1.0 - f, v=v, vb=v.astype(BF16), cum=cum)

    def split(p):
        c3 = p["cum"].reshape(n_sub, SUB, dim)
        start = jnp.concatenate([jnp.zeros((1, 1, dim), F32), c3[:n_sub - 1, SUB - 1:SUB, :]], axis=0)
        p.update(c3=c3, start=start, b3=c3 - start, c_end=c3[n_sub - 1, SUB - 1:SUB, :],
                 q3=p["q"].reshape(n_sub, SUB, dim), k3=p["kk"].reshape(n_sub, SUB, dim))

    def first_products(p):
        p["upd"] = lax.dot_general(p["vb"], (p["kk"] * jnp.exp2(p["c_end"] - p["cum"])).astype(BF16),
                                   (((0,), (0,)), ((), ())), preferred_element_type=F32)
        qt3 = (p["q3"] * jnp.exp2(p["b3"])).astype(BF16)
        p["scores"] = []
        for i in range(1, n_sub):
            kt = (p["k3"][:i] * jnp.exp2(p["start"][i:i + 1] - p["c3"][:i])).reshape(i * SUB, dim).astype(BF16)
            p["scores"].append(lax.dot_general(qt3[i], kt, trans_b, preferred_element_type=F32).astype(BF16))

    def within_sub_chunks(p, slot):
        b3, q3 = p["b3"], p["q3"]
        shift_s[slot] = (b3 - jnp.log2(p["k3"])).reshape(LANES, dim)
        vrow_s[slot] = p["v"]
        key_row = lambda ref, s: jnp.concatenate(
            [jnp.broadcast_to(ref[slot, c * SUB + s:c * SUB + s + 1, :], (1, half, dim)) for c in range(n_sub)], axis=0)
        b_lo, b_hi, q_lo, q_hi = b3[:, :half], b3[:, half:], q3[:, :half], q3[:, half:]
        od_lo = jnp.zeros((n_sub, half, dim), F32)
        od_hi = jnp.zeros((n_sub, half, dim), F32)
        for s in range(SUB):
            bs, vs = key_row(shift_s, s), key_row(vrow_s, s)
            w_hi = jnp.sum(q_hi * jnp.exp2(b_hi - bs), axis=-1, keepdims=True)
            if s < half:
                w_lo = jnp.sum(q_lo * jnp.exp2(b_lo - bs), axis=-1, keepdims=True)
                od_lo = od_lo + jnp.where(half_row >= s, w_lo, 0.0) * vs
                od_hi = od_hi + w_hi * vs
            else:
                od_hi = od_hi + jnp.where(half_row >= s - half, w_hi, 0.0) * vs
        p["o_sub"] = jnp.concatenate([od_lo, od_hi], axis=1).reshape(LANES, dim)

    def tiles(g, st):
        group = [prepare(g * TILE_UNROLL + u) for u in range(TILE_UNROLL)]
        for p in group:
            split(p)
            first_products(p)
        for p in group:
            p["o_state"] = lax.dot_general((p["q"] * jnp.exp2(p["cum"])).astype(BF16), st.astype(BF16), trans_b,
                                           preferred_element_type=F32)
            st = st * jnp.exp2(p["c_end"]) + p["upd"]
        for u, p in enumerate(group):
            within_sub_chunks(p, u)
        for p in group:
            cross = [jnp.dot(a, p["vb"][0:(i + 1) * SUB], preferred_element_type=F32) for i, a in enumerate(p["scores"])]
            o = p["o_sub"] + p["o_state"] + jnp.concatenate([jnp.zeros((SUB, dim), F32)] + cross, axis=0)
            o_ref[0, p["rows"], :] = _gated_out(o, gw, og_ref[0, p["rows"], :])
        return st

    st = lax.fori_loop(0, t_len // (LANES * TILE_UNROLL), tiles, jnp.zeros((dim, dim), F32))
    s_ref[0, 0] = st.T


def _hgrn_tri():
    r = np.arange(LANES)
    return jnp.asarray((r[None, :] <= r[:, None]).astype(np.float32), BF16)


def _hgrn_prompt(hg, lb_logits, gw, phys, p_sel, p_own, v_new, cache_vt):
    b, s, _ = hg.shape
    db = p_sel.shape[0]
    assert s % (LANES * TILE_UNROLL) == 0 and db % (b * HGRN_HEADS) == 0
    spp = db // (b * HGRN_HEADS)
    col = lambda part: pl.BlockSpec((1, s, HGRN_DIM), lambda i, h, pp, part=part: (i, 0, part * HGRN_HEADS + h))
    const = lambda shape: pl.BlockSpec(shape, lambda i, h, pp: (0,) * len(shape), pipeline_mode=pl.Buffered(1))
    per_sample = lambda w: pl.BlockSpec((spp, ATTN_HEADS, 1, w), lambda i, h, pp: (i * HGRN_HEADS + h, 0, 0, 0))
    return pl.pallas_call(
        functools.partial(_hgrn_prompt_kernel, spp),
        grid_spec=pltpu.PrefetchScalarGridSpec(
            num_scalar_prefetch=1,
            grid=(b, HGRN_HEADS),
            in_specs=[col(0), col(1), col(2), col(3),
                      pl.BlockSpec((lb_logits.shape[0], HGRN_DIM), lambda i, h, pp: (0, h)),
                      const(gw.shape), const((LANES, LANES)),
                      per_sample(p_sel.shape[3]), per_sample(LANES), per_sample(ATTN_HEAD_DIM),
                      pl.BlockSpec(memory_space=pl.ANY)],
            out_specs=[pl.BlockSpec((1, s, HGRN_DIM), lambda i, h, pp: (i, 0, h)),
                       pl.BlockSpec((1, 1, HGRN_DIM, HGRN_DIM), lambda i, h, pp: (i, h, 0, 0)),
                       per_sample(ATTN_HEAD_DIM)],
            scratch_shapes=[pltpu.VMEM((TILE_UNROLL, LANES, HGRN_DIM), F32), pltpu.VMEM((TILE_UNROLL, LANES, HGRN_DIM), F32),
                            pltpu.VMEM((2, spp * N_SEL_TILES, ATTN_HEAD_DIM, PAGE_SIZE), F32),
                            pltpu.SemaphoreType.DMA((2, spp))]),
        out_shape=[jax.ShapeDtypeStruct((b, s, HGRN_WIDTH), F32),
                   jax.ShapeDtypeStruct((b, HGRN_HEADS, HGRN_DIM, HGRN_DIM), F32),
                   jax.ShapeDtypeStruct((db, ATTN_HEADS, 1, ATTN_HEAD_DIM), F32)],
        compiler_params=_params(2),
        name="hgrn_prompt",
    )(phys, hg, hg, hg, hg, lb_logits, gw, _hgrn_tri(), p_sel, p_own, v_new, cache_vt)


def _hgrn_sample_kernel(xt_ref, hg_ref, lblt_ref, gw_ref, st_ref, o_ref, so_ref):
    width = HGRN_WIDTH
    xt = xt_ref[0]
    lane = lax.broadcasted_iota(I32, xt.shape, 1)
    lb = _lower_bound(lblt_ref[...], 1)
    for u in range(st_ref.shape[0]):
        n = pl.program_id(0) * st_ref.shape[0] + u
        cols = jnp.sum(jnp.where(lane == n, xt, 0.0), axis=1, keepdims=True)
        hg = hg_ref[u]
        for h in range(HGRN_HEADS):
            rows = slice(h * HGRN_DIM, (h + 1) * HGRN_DIM)
            qr = cols[rows]
            q = qr * _sigmoid(qr)
            lbh = lb[rows]
            f = lbh + (1.0 - lbh) * _sigmoid(cols[width + h * HGRN_DIM:width + (h + 1) * HGRN_DIM])
            v = hg[:, 2 * width + h * HGRN_DIM:2 * width + (h + 1) * HGRN_DIM]
            og = hg[:, 3 * width + h * HGRN_DIM:3 * width + (h + 1) * HGRN_DIM]
            s_new = f * st_ref[u, h] + (1.0 - f) * v
            so_ref[u, h] = s_new
            o = jnp.sum(q * s_new, axis=0, keepdims=True)
            o_ref[u, :, rows] = _gated_out(o, gw_ref[...], og)


def _hgrn_sample(xt, hg, lb_logits_t, gw, state):
    db = state.shape[0]
    ns = SAMPLES_PER_STEP
    assert db % ns == 0
    return pl.pallas_call(
        _hgrn_sample_kernel,
        grid=(db // ns,),
        in_specs=[_const_spec(xt.shape),
                  pl.BlockSpec((ns, 1, hg.shape[2]), lambda n: (n, 0, 0)),
                  _const_spec(lb_logits_t.shape), _const_spec(gw.shape),
                  pl.BlockSpec((ns,) + state.shape[1:], lambda n: (n, 0, 0, 0))],
        out_specs=[pl.BlockSpec((ns, 1, HGRN_WIDTH), lambda n: (n, 0, 0)),
                   pl.BlockSpec((ns,) + state.shape[1:], lambda n: (n, 0, 0, 0))],
        out_shape=[jax.ShapeDtypeStruct((db, 1, HGRN_WIDTH), F32),
                   jax.ShapeDtypeStruct(state.shape, F32)],
        compiler_params=_params(1),
        name="hgrn_sample",
    )(xt, hg, lb_logits_t, gw, state)


def _post_kernel(x_ref, oa_ref, oh_ref, gt_ref, ada_ref, wua_ref, wuh_ref, wo_ref, nf_ref, w1_ref, w2_ref, fn_ref, y_ref):
    d = x_ref.shape[-1]
    ada = ada_ref[0]
    g = gt_ref[0]
    ua = jnp.dot(oa_ref[0].astype(BF16), wua_ref[...], preferred_element_type=F32)
    uh = jnp.dot(oh_ref[0].astype(BF16), wuh_ref[...], preferred_element_type=F32)
    merged = _sigmoid(g[:, :d]) * ua + _sigmoid(g[:, d:]) * uh
    x1 = x_ref[0] + ada[:, 2 * d:3 * d] * jnp.dot(merged.astype(BF16), wo_ref[...], preferred_element_type=F32)
    h2 = _rms_mod(x1, nf_ref[...], ada[:, 3 * d:4 * d], ada[:, 4 * d:5 * d]).astype(BF16)
    a = jnp.maximum(jnp.dot(h2, w1_ref[...], preferred_element_type=F32), 0.0)
    x2 = x1 + ada[:, 5 * d:6 * d] * jnp.dot((a * a).astype(BF16), w2_ref[...], preferred_element_type=F32)
    var = jnp.mean(x2 * x2, axis=-1, keepdims=True)
    y_ref[0] = x2 * lax.rsqrt(var + EPS) * fn_ref[...]


def _post(x, oa, oh, gt, ada, wua, wuh, wo, nf, w1, w2, fn, tm):
    b, s, d = x.shape
    r = ada.shape[1]
    assert s % tm == 0 and r in (1, s) and (r == 1 or tm == s)
    row = lambda w: pl.BlockSpec((1, tm, w), lambda i, j: (i, j, 0))
    return pl.pallas_call(
        _post_kernel,
        grid=(b, s // tm),
        in_specs=[row(d), row(oa.shape[2]), row(oh.shape[2]), row(gt.shape[2]),
                  pl.BlockSpec((1, r if r == 1 else tm, N_ADA * d), lambda i, j: (i, 0, 0)),
                  _const_spec(wua.shape), _const_spec(wuh.shape), _const_spec(wo.shape), _const_spec(nf.shape),
                  _const_spec(w1.shape), _const_spec(w2.shape), _const_spec(fn.shape)],
        out_specs=row(d),
        out_shape=jax.ShapeDtypeStruct((b, s, d), F32),
        compiler_params=_params(2),
        name="post",
    )(x, oa, oh, gt, ada, wua, wuh, wo, nf, w1, w2, fn)


def _sample_query(qt_ref, n, qb_ref):
    qt = qt_ref[0]
    pick = lax.broadcasted_iota(I32, qt.shape, 1) == n
    qb_ref[...] = jnp.broadcast_to(jnp.sum(jnp.where(pick, qt, 0.0), axis=1, keepdims=True), qb_ref.shape)


def _page_scores(pages_ref, qb_ref, sc_ref, first_page):
    for h in range(ATTN_HEADS):
        rows = slice(h * ATTN_HEAD_DIM, (h + 1) * ATTN_HEAD_DIM)
        qh = qb_ref[rows, :]
        for j in range(PAGES_PER_STEP):
            lanes = slice((first_page + j) * PAGE_SIZE, (first_page + j + 1) * PAGE_SIZE)
            sc_ref[h:h + 1, lanes] = jnp.sum(pages_ref[j, rows, :] * qh, axis=0, keepdims=True)


def _select_blocks(sc, n, qt_ref, kt_ref, ts_ref, ps_ref, idx_ref, po_ref, u):
    n_blk = sc.shape[1] // MOBA_BLOCK
    lane = lax.broadcasted_iota(I32, (ATTN_HEADS, LANES), 1).astype(F32)
    bs = jnp.full((ATTN_HEADS, LANES), -jnp.inf, F32)
    for b in range(n_blk):
        tot = jnp.sum(sc[:, b * MOBA_BLOCK:(b + 1) * MOBA_BLOCK], axis=1, keepdims=True)
        bs = jnp.where(lane == float(b), tot * (1.0 / MOBA_BLOCK), bs)
    picks = []
    for _ in range(MOBA_TOPK):
        m = jnp.max(bs, axis=1, keepdims=True)
        i = jnp.min(jnp.where(bs == m, lane, float(LANES)), axis=1, keepdims=True)
        picks.append(i)
        bs = jnp.where(lane == i, -jnp.inf, bs)
    ts = ts_ref[...].reshape(ATTN_HEADS, MOBA_BLOCK + LANES)
    logits = []
    for i in picks:
        blk = jnp.zeros((ATTN_HEADS, MOBA_BLOCK), F32)
        for b in range(n_blk):
            blk = jnp.where(i == float(b), sc[:, b * MOBA_BLOCK:(b + 1) * MOBA_BLOCK], blk)
        logits.append(blk * SCALE + jnp.where(i == float(n_blk - 1), ts[:, :MOBA_BLOCK], 0.0))
    lg = jnp.concatenate(logits, axis=1)
    prod = qt_ref[0] * kt_ref[0].reshape(ATTN_WIDTH, -1)
    own_all = jnp.sum(prod.reshape(ATTN_HEADS, ATTN_HEAD_DIM, prod.shape[1]), axis=1)
    pick = lax.broadcasted_iota(I32, own_all.shape, 1) == n
    own = jnp.sum(jnp.where(pick, own_all, 0.0), axis=1, keepdims=True) * SCALE + ts[:, MOBA_BLOCK:MOBA_BLOCK + 1]
    m = jnp.maximum(jnp.max(lg, axis=1, keepdims=True), own)
    e = jnp.exp(lg - m)
    eo = jnp.exp(own - m)
    den = jnp.sum(e, axis=1, keepdims=True) + eo
    ps_ref[u] = e / den
    po_ref[u] = jnp.broadcast_to(eo / den, (ATTN_HEADS, LANES))
    idx = jnp.zeros((ATTN_HEADS, LANES), F32)
    for t, i in enumerate(picks):
        idx = jnp.where(lane == float(t), i, idx)
    idx_ref[u] = idx.astype(I32)


N_SEL_PAGES = MOBA_TOPK * (MOBA_BLOCK // PAGE_SIZE)


N_SEL_TILES = ATTN_HEADS * N_SEL_PAGES


def _value_copies(pp_ref, cache_ref, vpages_ref, sem, n, u):
    page = lambda h, j: 0 if n is None else pp_ref[(n * ATTN_HEADS + h) * N_SEL_PAGES + j]
    return [pltpu.make_async_copy(cache_ref.at[page(h, j), h],
                                  vpages_ref.at[u * N_SEL_TILES + h * N_SEL_PAGES + j], sem.at[u])
            for h in range(ATTN_HEADS) for j in range(N_SEL_PAGES)]


def _weighted_values(vpages_ref, ps_ref, po_ref, vn_ref, o_ref, u):
    acc = []
    for h in range(ATTN_HEADS):
        ps = ps_ref[u, h]
        a = jnp.zeros((ATTN_HEAD_DIM, PAGE_SIZE), F32)
        for j in range(N_SEL_PAGES):
            a = a + vpages_ref[u * N_SEL_TILES + h * N_SEL_PAGES + j] * ps[:, j * PAGE_SIZE:(j + 1) * PAGE_SIZE]
        acc.append(a)
    o = lax.dot_general(jnp.ones((SUBLANES, PAGE_SIZE), F32), jnp.concatenate(acc, axis=0), (((1,), (1,)), ((), ())),
                        preferred_element_type=F32, precision=HIGHEST)
    for h in range(ATTN_HEADS):
        o_ref[u, h] = (o[0:1, h * ATTN_HEAD_DIM:(h + 1) * ATTN_HEAD_DIM]
                       + po_ref[u, h][:, 0:ATTN_HEAD_DIM] * vn_ref[u, h])


def kernel(x_prompt, x_sample, cache_k, cache_v, state_hgrn, page_table, c_prompt, c_sample, rel_bias, hgrn_lb_logits,
           w_ada, b_ada, norm_mix_w, w_in, hgrn_gnorm_w, w_up_attn, w_up_hgrn, w_out, norm_ffn_w, w_ff1, w_ff2, final_norm_w):
    depth = w_in.shape[0]
    assert depth == 1, "one-layer trunk"
    b, s, d = x_prompt.shape
    db, ds, _ = x_sample.shape
    assert ds == 1
    past = page_table.shape[1] * PAGE_SIZE

    w = w_in[0]
    a0, a1, a2 = ATTN_WIDTH, 2 * ATTN_WIDTH, 3 * ATTN_WIDTH
    h0 = a2 + 4 * HGRN_WIDTH
    wq = w[:, :a0].astype(BF16)
    wh = w[:, a2:h0].astype(BF16)
    wg = w[:, h0:].astype(BF16)
    wt_kv = w[:, a0:a2].T.astype(BF16)
    wt_s = jnp.concatenate([w[:, a0:a2], w[:, :a0], w[:, a2:a2 + 2 * HGRN_WIDTH]], axis=1).T.astype(BF16)
    wua, wuh, wo = w_up_attn[0].astype(BF16), w_up_hgrn[0].astype(BF16), w_out[0].astype(BF16)
    w1, w2 = w_ff1[0].astype(BF16), w_ff2[0].astype(BF16)
    fn = final_norm_w.reshape(1, d)

    ada = _ada(jnp.concatenate([c_prompt, c_sample], axis=0), w_ada[0], b_ada)
    ada_p = ada[:b].reshape(b, 1, N_ADA * d)
    ada_s = ada[b:].reshape(1, db, N_ADA * d)
    t_own, t_prev, t_samp = _bias_tables(rel_bias)

    xs = x_sample.reshape(1, db, d)
    qa_s, hg_s, gt_s, kt_s, vt_s, qt_s, qht_s, fht_s = _inproj(xs, ada_s, norm_mix_w, wq, wh, wg, wt_s, db)
    qa, hg, gt, kt, vt = _inproj(x_prompt, ada_p, norm_mix_w, wq, wh, wg, wt_kv, INPROJ_ROW_TILE)
    cache_kt = jnp.transpose(cache_k[0], (0, 2, 3, 1))
    cache_vt = jnp.transpose(cache_v[0], (0, 2, 3, 1))

    o_attn, p_sel, idx, p_own = _attention(qa, kt, vt, t_own, t_prev, page_table, qt_s, kt_s, t_samp,
                                           cache_kt.reshape(-1, ATTN_WIDTH, PAGE_SIZE))
    ppb = MOBA_BLOCK // PAGE_SIZE
    logical = idx[:, :, :MOBA_TOPK, None] * ppb + jnp.arange(ppb, dtype=I32)
    phys = jnp.take_along_axis(page_table, logical.reshape(db, -1), axis=1).reshape(-1)
    v_new = jnp.transpose(vt_s[0], (2, 0, 1))[:, :, None, :]
    o_hgrn, state_p, o_attn_s = _hgrn_prompt(hg, hgrn_lb_logits, hgrn_gnorm_w, phys, p_sel[:, :, None, :],
                                             p_own[:, :, None, :], v_new, cache_vt)
    y_prompt = _post(x_prompt, o_attn, o_hgrn, gt, ada_p, wua, wuh, wo, norm_ffn_w, w1, w2, fn, ROW_TILE)

    o_attn_s = o_attn_s.reshape(1, db, ATTN_WIDTH)
    xt = jnp.concatenate([qht_s, fht_s], axis=1)
    o_hgrn_s, state_s = _hgrn_sample(xt, hg_s.reshape(db, 1, -1), hgrn_lb_logits.T, hgrn_gnorm_w, state_hgrn[0])
    y_sample = _post(xs, o_attn_s, o_hgrn_s.reshape(1, db, HGRN_WIDTH), gt_s, ada_s, wua, wuh, wo, norm_ffn_w, w1, w2, fn, db)

    to_rows = lambda t: jnp.transpose(t, (0, 3, 1, 2))[None]
    return (y_prompt, y_sample.reshape(db, 1, d), to_rows(kt), to_rows(vt), state_p[None],
            to_rows(kt_s).reshape(1, db, 1, ATTN_HEADS, ATTN_HEAD_DIM), to_rows(vt_s).reshape(1, db, 1, ATTN_HEADS, ATTN_HEAD_DIM),
            state_s[None])
```
